```python
import math
import jax, jax.numpy as jnp
from jax import lax
import numpy as np

D_MODEL = 1024
BATCH = 16
SEQ = 2048
DEPTH = 2

HEAD_DIM = 64
A_HEADS = 6
A_KV_HEADS = 2
B_HEADS = 6
C_HEADS = 4
C_KV_HEADS = 2
MIX_WIDTH = HEAD_DIM * (A_HEADS + B_HEADS + C_HEADS)
PROJ_WIDTHS = (A_HEADS * HEAD_DIM, A_KV_HEADS * HEAD_DIM, A_KV_HEADS * HEAD_DIM,
               B_HEADS * HEAD_DIM, B_HEADS * HEAD_DIM, B_HEADS * HEAD_DIM,
               C_HEADS * HEAD_DIM, C_KV_HEADS * HEAD_DIM, C_KV_HEADS * HEAD_DIM)
PROJ_WIDTH = sum(PROJ_WIDTHS)
ROPE_THETA = 10000.0
GRID_W = 64
Q_BLOCK = 128
DILATED_PATTERNS = ((128, 1), (512, 4), (2048, 16))
LOCAL_HALF_WINDOW = 128
LOCAL_BLOCK = 128
PEER_HEADS = 8
N_KEYS = 128
N_EXPERTS = N_KEYS * N_KEYS
PEER_TOPK = 16
PEER_DK = 128
PEER_CHUNK = 128
NORM_EPS = 1e-6
NEG_INF = -1e30
ATTN_SCALE = HEAD_DIM ** -0.5

kernel_name = "hybrid_parallel_heads_peer_encoder"


def _rms(x):
    xf = x.astype(jnp.float32)
    y = xf * lax.rsqrt(jnp.mean(xf * xf, axis=-1, keepdims=True) + NORM_EPS)
    return y.astype(x.dtype)


def rms_norm(x, g):
    return _rms(x) * g.astype(x.dtype)


def _inv_freq(dim):
    return ROPE_THETA ** (-jnp.arange(0, dim, 2, dtype=jnp.float32) / dim)


def _rope(x, ang):
    half = x.shape[-1] // 2
    c = jnp.cos(ang)[None, :, None, :].astype(x.dtype)
    s = jnp.sin(ang)[None, :, None, :].astype(x.dtype)
    x1, x2 = x[..., :half], x[..., half:]
    return jnp.concatenate([x1 * c - x2 * s, x1 * s + x2 * c], axis=-1)


def rope_1d(x):
    pos = jnp.arange(x.shape[1], dtype=jnp.float32)
    return _rope(x, pos[:, None] * _inv_freq(HEAD_DIM)[None, :])


def rope_axial(x):
    seq = x.shape[1]
    rows = seq // GRID_W
    row = jnp.repeat(jnp.arange(rows, dtype=jnp.float32), GRID_W)
    col = jnp.tile(jnp.arange(GRID_W, dtype=jnp.float32), rows)
    half = HEAD_DIM // 2
    f = _inv_freq(half)
    return jnp.concatenate([_rope(x[..., :half], row[:, None] * f[None, :]),
                            _rope(x[..., half:], col[:, None] * f[None, :])], axis=-1)


def full_gqa(q, k, v):
    bsz, seq, h, dh = q.shape
    kv = k.shape[2]
    g = h // kv
    nb = seq // Q_BLOCK
    qb = q.reshape(bsz, nb, Q_BLOCK, kv, g, dh).transpose(1, 0, 2, 3, 4, 5)

    def block(qi):
        s = jnp.einsum('bqkgd,bskd->bkgqs', qi, k).astype(jnp.float32) * ATTN_SCALE
        p = jax.nn.softmax(s, axis=-1)
        return jnp.einsum('bkgqs,bskd->bqkgd', p.astype(v.dtype), v)

    o = lax.map(block, qb)
    return o.transpose(1, 0, 2, 3, 4, 5).reshape(bsz, seq, h, dh)


def banded_attn(q, k, v, half_window, block, sink=None):
    n, length, h, dh = q.shape
    kv = k.shape[2]
    g = h // kv
    nb = -(-length // block)
    lp = nb * block
    pad = lp - length
    qp = jnp.pad(q, ((0, 0), (0, pad), (0, 0), (0, 0))).reshape(n, nb, block, kv, g, dh)

    def kv_blocks(t):
        tp = jnp.pad(t, ((0, 0), (block, pad + block), (0, 0), (0, 0))).reshape(n, nb + 2, block, kv, dh)
        return jnp.concatenate([tp[:, :-2], tp[:, 1:-1], tp[:, 2:]], axis=2)

    kb, vb = kv_blocks(k), kv_blocks(v)
    qi = jnp.arange(lp).reshape(nb, block)
    kj = (jnp.arange(nb)[:, None] - 1) * block + jnp.arange(3 * block)[None, :]
    mask = ((jnp.abs(qi[:, :, None] - kj[:, None, :]) <= half_window)
            & (kj[:, None, :] >= 0) & (kj[:, None, :] < length))
    s = jnp.einsum('nbqkgd,nbckd->nbkgqc', qp, kb).astype(jnp.float32) * ATTN_SCALE
    s = jnp.where(mask[None, :, None, None], s, NEG_INF)
    m = jnp.max(s, axis=-1)
    if sink is not None:
        sink_b = sink.astype(jnp.float32).reshape(1, 1, kv, g, 1)
        m = jnp.maximum(m, sink_b)
    p = jnp.exp(s - m[..., None])
    l = jnp.sum(p, axis=-1)
    if sink is not None:
        l = l + jnp.exp(sink_b - m)
    o = jnp.einsum('nbkgqc,nbckd->nbqkgd', p.astype(vb.dtype), vb).astype(jnp.float32)
    o = o / l.transpose(0, 1, 4, 2, 3)[..., None]
    lse = (m + jnp.log(l)).transpose(0, 1, 4, 2, 3).reshape(n, lp, h)
    o = o.reshape(n, lp, h, dh)
    return o[:, :length].astype(q.dtype), lse[:, :length]


def dilated_attn(q, k, v):
    bsz, seq, h, dh = q.shape
    outs, lses = [], []
    for window, r in DILATED_PATTERNS:
        length = seq // r
        half = window // (2 * r)

        def to_sub(t):
            return t.reshape(bsz, length, r, h, dh).transpose(0, 2, 1, 3, 4).reshape(bsz * r, length, h, dh)

        o, lse = banded_attn(to_sub(q), to_sub(k), to_sub(v), half, half)
        outs.append(o.reshape(bsz, r, length, h, dh).transpose(0, 2, 1, 3, 4).reshape(bsz, seq, h, dh))
        lses.append(lse.reshape(bsz, r, length, h).transpose(0, 2, 1, 3).reshape(bsz, seq, h))
    alpha = jax.nn.softmax(jnp.stack(lses, axis=0), axis=0)
    o = jnp.sum(alpha[..., None] * jnp.stack(outs, axis=0).astype(jnp.float32), axis=0)
    return o.astype(q.dtype)


def peer_ffn(h, wq, keys, u, v):
    bsz, seq, d = h.shape
    hc = h.reshape((bsz * seq) // PEER_CHUNK, PEER_CHUNK, d)

    def chunk(xc):
        q = (xc @ wq).reshape(PEER_CHUNK, PEER_HEADS, 2, PEER_DK // 2)
        s = jnp.einsum('chpd,hpnd->chpn', q, keys).astype(jnp.float32)
        sv, si = lax.top_k(s, PEER_TOPK)
        cand = sv[:, :, 0, :, None] + sv[:, :, 1, None, :]
        cidx = si[:, :, 0, :, None] * N_KEYS + si[:, :, 1, None, :]
        cand = cand.reshape(PEER_CHUNK, PEER_HEADS, PEER_TOPK * PEER_TOPK)
        cidx = cidx.reshape(PEER_CHUNK, PEER_HEADS, PEER_TOPK * PEER_TOPK)
        best, pos = lax.top_k(cand, PEER_TOPK)
        idx = jnp.take_along_axis(cidx, pos, axis=-1)
        gate = jax.nn.softmax(best, axis=-1)
        ue = jnp.take(u, idx, axis=0)
        act = jax.nn.gelu(jnp.einsum('cd,chkd->chk', xc, ue).astype(jnp.float32), approximate=False) * gate
        ve = jnp.take(v, idx, axis=0)
        return jnp.einsum('chk,chkd->cd', act.astype(xc.dtype), ve)

    return lax.map(chunk, hc).reshape(bsz, seq, d)


def setup_inputs(seed: int = 0) -> dict:
    key = jax.random.key(seed)
    ks = jax.random.split(key, 12)
    f32 = jnp.float32
    nrm = lambda k, shape: jax.random.normal(k, shape, dtype=f32)
    return {
        "x": nrm(ks[0], (BATCH, SEQ, D_MODEL)),
        "attn_norm": 1.0 + 0.02 * nrm(ks[1], (DEPTH, D_MODEL)),
        "w_in": nrm(ks[2], (DEPTH, D_MODEL, PROJ_WIDTH)) * D_MODEL ** -0.5,
        "qk_gain": 1.0 + 0.02 * nrm(ks[3], (DEPTH, 3, 2, HEAD_DIM)),
        "sink_logits": 0.5 * nrm(ks[4], (DEPTH, C_HEADS)),
        "out_norm": 1.0 + 0.02 * nrm(ks[5], (DEPTH, MIX_WIDTH)),
        "w_out": nrm(ks[6], (DEPTH, MIX_WIDTH, D_MODEL)) * MIX_WIDTH ** -0.5,
        "ffn_norm": 1.0 + 0.02 * nrm(ks[7], (DEPTH, D_MODEL)),
        "peer_wq": nrm(ks[8], (DEPTH, D_MODEL, PEER_HEADS * PEER_DK)) * D_MODEL ** -0.5,
        "peer_keys": nrm(ks[9], (DEPTH, PEER_HEADS, 2, N_KEYS, PEER_DK // 2)) * (PEER_DK // 2) ** -0.5,
        "peer_u": nrm(ks[10], (DEPTH, N_EXPERTS, D_MODEL)) * D_MODEL ** -0.5,
        "peer_v": nrm(ks[11], (DEPTH, N_EXPERTS, D_MODEL)) * PEER_HEADS ** -0.5,
    }


def reference(x, attn_norm, w_in, qk_gain, sink_logits, out_norm, w_out, ffn_norm,
              peer_wq, peer_keys, peer_u, peer_v):
    bsz, seq, _ = x.shape
    bounds = [int(b) for b in np.cumsum(PROJ_WIDTHS)[:-1]]

    def heads(t):
        return t.reshape(bsz, seq, -1, HEAD_DIM)

    for layer in range(DEPTH):
        h = rms_norm(x, attn_norm[layer])
        proj = jnp.einsum('bsd,dp->bsp', h, w_in[layer])
        qa, ka, va, qb, kb, vb, qc, kc, vc = jnp.split(proj, bounds, axis=-1)
        g = qk_gain[layer]
        qa = rope_axial(rms_norm(heads(qa), g[0, 0]))
        ka = rope_axial(rms_norm(heads(ka), g[0, 1]))
        oa = full_gqa(qa, ka, heads(va))
        qb = rope_1d(rms_norm(heads(qb), g[1, 0]))
        kb = rope_1d(rms_norm(heads(kb), g[1, 1]))
        ob = dilated_attn(qb, kb, heads(vb))
        qc = rope_1d(rms_norm(heads(qc), g[2, 0]))
        kc = rope_1d(rms_norm(heads(kc), g[2, 1]))
        oc, _ = banded_attn(qc, kc, heads(vc), LOCAL_HALF_WINDOW, LOCAL_BLOCK, sink_logits[layer])
        mix = jnp.concatenate([_rms(oa.reshape(bsz, seq, -1)),
                               _rms(ob.reshape(bsz, seq, -1)),
                               _rms(oc.reshape(bsz, seq, -1))], axis=-1) * out_norm[layer].astype(x.dtype)
        x = x + jnp.einsum('bsm,md->bsd', mix, w_out[layer])
        x = x + peer_ffn(rms_norm(x, ffn_norm[layer]), peer_wq[layer], peer_keys[layer],
                         peer_u[layer], peer_v[layer])
    return x
```

```python
import functools

import numpy as np
import jax
import jax.numpy as jnp
from jax import lax
from jax.experimental import pallas as pl
from jax.experimental.pallas import tpu as pltpu

F32 = jnp.float32
BF16 = jnp.bfloat16

LANES = 128
HEAD_DIM = 64
A_HEADS, A_KV_HEADS = 6, 2
B_HEADS = 6
C_HEADS, C_KV_HEADS = 4, 2
ROPE_THETA = 10000.0
GRID_W = 64
DILATIONS = ((128, 1), (512, 4), (2048, 16))
LOCAL_HALF_WINDOW = 128
PEER_HEADS = 8
N_KEYS = 128
PEER_TOPK = 16
NORM_EPS = 1e-6
NEG_INF = -1e30
ATTN_SCALE = HEAD_DIM ** -0.5

TOKEN_TILE = 256
PEER_TOKENS = 32
VMEM_LIMIT = 48 * 1024 * 1024
PEER_VMEM_LIMIT = 56 * 1024 * 1024

_GROUPS = ([("a", "q", "axial")] * 3 + [("a", "k", "axial")] + [("a", "v", None)]
           + [("b", "q", "1d")] * 3 + [("b", "k", "1d")] * 3 + [("b", "v", None)] * 3
           + [("c", "q", "1d")] * 2 + [("c", "k", "1d")] + [("c", "v", None)])
N_GROUPS = len(_GROUPS)
A_Q_ORDER = (0, 3, 1, 4, 2, 5)
C_Q_ORDER = (0, 2, 1, 3)


def _rms_rows(t):
    return t * lax.rsqrt(jnp.mean(t * t, axis=-1, keepdims=True) + NORM_EPS)


def _proj_kernel(x_ref, g_ref, w_ref, bd_ref, gain_ref, cos_ref, sin_ref,
                 aq_ref, ak_ref, av_ref, b1_ref, b4_ref, b16_ref, cq_ref, ck_ref, cv_ref,
                 scr_ref, *, tm):
    x = x_ref[...]
    h = _rms_rows(x) * g_ref[...]
    proj = jnp.dot(h.astype(BF16), w_ref[...], preferred_element_type=F32)
    bd = bd_ref[...]
    lane = lax.broadcasted_iota(jnp.int32, (1, LANES), 1)

    def norm_rope(y, j, kind):
        sq = y * y
        hi = sq.astype(BF16)
        lo = (sq - hi.astype(F32)).astype(BF16)
        ssq = (jnp.dot(hi, bd, preferred_element_type=F32)
               + jnp.dot(lo, bd, preferred_element_type=F32))
        yn = y * lax.rsqrt(ssq * (1.0 / HEAD_DIM) + NORM_EPS) * gain_ref[j:j + 1, :]
        t, sh = (0, 16) if kind == "axial" else (1, 32)
        first = (lane % (2 * sh)) < sh
        partner = jnp.where(first, pltpu.roll(yn, LANES - sh, 1), pltpu.roll(yn, sh, 1))
        return yn * cos_ref[t] + partner * sin_ref[t]

    outs = {("a", "q"): aq_ref, ("a", "k"): ak_ref, ("a", "v"): av_ref,
            ("c", "q"): cq_ref, ("c", "k"): ck_ref, ("c", "v"): cv_ref}
    pos = {}
    b_col = 0
    for j, (mixer, role, kind) in enumerate(_GROUPS):
        y = proj[:, j * LANES:(j + 1) * LANES]
        if kind is not None:
            y = norm_rope(y, j, kind)
        if mixer == "b":
            scr_ref[b_col] = y
            b1_ref[:, b_col * LANES:(b_col + 1) * LANES] = y.astype(BF16)
            b_col += 1
        else:
            c = pos.get((mixer, role), 0)
            outs[(mixer, role)][:, c * LANES:(c + 1) * LANES] = y.astype(BF16)
            pos[(mixer, role)] = c + 1
    for r, ref in ((4, b4_ref), (16, b16_ref)):
        for c in range(r):
            for g in range(b_col):
                ref[c, :, g * LANES:(g + 1) * LANES] = scr_ref[g, pl.ds(c, tm // r, stride=r), :].astype(BF16)


def _proj_call(x2d, g, w, bd, gains, cos_t, sin_t, *, batch, seq):
    n = x2d.shape[0]
    tm = TOKEN_TILE
    nt = seq // tm
    bw = 9 * LANES
    row = lambda i: (i, 0)
    const2 = lambda i: (0, 0)
    bmap = lambda i: (i // nt, 0, i % nt, 0)
    outs = [
        (jax.ShapeDtypeStruct((n, 3 * LANES), BF16), pl.BlockSpec((tm, 3 * LANES), row)),
        (jax.ShapeDtypeStruct((n, LANES), BF16), pl.BlockSpec((tm, LANES), row)),
        (jax.ShapeDtypeStruct((n, LANES), BF16), pl.BlockSpec((tm, LANES), row)),
        (jax.ShapeDtypeStruct((batch, 1, seq, bw), BF16), pl.BlockSpec((None, None, tm, bw), bmap)),
        (jax.ShapeDtypeStruct((batch, 4, seq // 4, bw), BF16), pl.BlockSpec((None, 4, tm // 4, bw), bmap)),
        (jax.ShapeDtypeStruct((batch, 16, seq // 16, bw), BF16), pl.BlockSpec((None, 16, tm // 16, bw), bmap)),
        (jax.ShapeDtypeStruct((n, 2 * LANES), BF16), pl.BlockSpec((tm, 2 * LANES), row)),
        (jax.ShapeDtypeStruct((n, LANES), BF16), pl.BlockSpec((tm, LANES), row)),
        (jax.ShapeDtypeStruct((n, LANES), BF16), pl.BlockSpec((tm, LANES), row)),
    ]
    return pl.pallas_call(
        functools.partial(_proj_kernel, tm=tm),
        name="proj",
        grid=(n // tm,),
        in_specs=[
            pl.BlockSpec((tm, x2d.shape[1]), row),
            pl.BlockSpec(g.shape, const2),
            pl.BlockSpec(w.shape, const2),
            pl.BlockSpec(bd.shape, const2),
            pl.BlockSpec(gains.shape, const2),
            pl.BlockSpec((2, tm, LANES), lambda i: (0, i % nt, 0)),
            pl.BlockSpec((2, tm, LANES), lambda i: (0, i % nt, 0)),
        ],
        out_specs=[o[1] for o in outs],
        out_shape=[o[0] for o in outs],
        scratch_shapes=[pltpu.VMEM((bw // LANES, tm, LANES), F32)],
        compiler_params=pltpu.CompilerParams(dimension_semantics=("arbitrary",), vmem_limit_bytes=VMEM_LIMIT),
    )(x2d, g, w, bd, gains, cos_t, sin_t)


def _attn_kernel(*refs, seq_len, tq, win, half, n_qp, n_kp, has_sink, emit_lse):
    it = iter(refs)
    q_ref, k_ref, v_ref = next(it), next(it), next(it)
    sink_ref = next(it) if has_sink else None
    o_ref = next(it)
    lse_ref = next(it) if emit_lse else None
    lane = lax.broadcasted_iota(jnp.int32, (1, LANES), 1)
    low = lane < HEAD_DIM

    def tile(i, carry):
        r0 = pl.multiple_of(i * tq, tq)
        if half is None or win == seq_len:
            ks = 0
        else:
            ks = pl.multiple_of(jnp.clip(r0 - half, 0, seq_len - win), half)
        if half is not None:
            d = (r0 - ks) + lax.broadcasted_iota(jnp.int32, (tq, win), 0) - lax.broadcasted_iota(jnp.int32, (tq, win), 1)
            visible = jnp.abs(d) <= half
        for j in range(n_qp):
            kj = j if n_kp > 1 else 0
            q = q_ref[pl.ds(r0, tq), j * LANES:(j + 1) * LANES]
            k = k_ref[pl.ds(ks, win), kj * LANES:(kj + 1) * LANES]
            v = v_ref[pl.ds(ks, win), kj * LANES:(kj + 1) * LANES]
            o_h, lse_h = [], []
            for hf in range(2):
                qm = jnp.where(low if hf == 0 else jnp.logical_not(low), q, jnp.zeros_like(q))
                s = lax.dot_general(qm, k, (((1,), (1,)), ((), ())), preferred_element_type=F32)
                if half is not None:
                    s = jnp.where(visible, s, NEG_INF)
                m = jnp.max(s, axis=-1, keepdims=True)
                if has_sink:
                    sink = sink_ref[2 * j + hf]
                    m = jnp.maximum(m, sink)
                p = jnp.exp(s - m)
                l = jnp.sum(p, axis=-1, keepdims=True)
                if has_sink:
                    l = l + jnp.exp(sink - m)
                o = jnp.dot(p.astype(BF16), v, preferred_element_type=F32)
                o_h.append(o / l)
                if emit_lse:
                    lse_h.append(m + jnp.log(l))
            o_ref[pl.ds(r0, tq), j * LANES:(j + 1) * LANES] = jnp.where(low, o_h[0], o_h[1])
            if emit_lse:
                lse_ref[pl.ds(r0, tq), j * LANES:(j + 1) * LANES] = jnp.where(low, lse_h[0], lse_h[1])
        return carry

    lax.fori_loop(0, seq_len // tq, tile, 0)


def _attn_call(name, q_arg, k_arg, v_arg, grid, q_spec, k_spec, v_spec, out_shape, out_spec, *,
               seq_len, tq, win, half, n_qp, n_kp, sink=None, emit_lse=False):
    in_specs = [q_spec, k_spec, v_spec]
    args = [q_arg, k_arg, v_arg]
    if sink is not None:
        in_specs.append(pl.BlockSpec(memory_space=pltpu.SMEM))
        args.append(sink)
    n_out = 2 if emit_lse else 1
    res = pl.pallas_call(
        functools.partial(_attn_kernel, seq_len=seq_len, tq=tq, win=win, half=half, n_qp=n_qp, n_kp=n_kp,
                          has_sink=sink is not None, emit_lse=emit_lse),
        name=name,
        grid=grid,
        in_specs=in_specs,
        out_specs=[out_spec] * n_out,
        out_shape=[out_shape] * n_out,
        compiler_params=pltpu.CompilerParams(dimension_semantics=("arbitrary",) * len(grid),
                                             vmem_limit_bytes=VMEM_LIMIT),
    )(*args)
    return res


def _top_rows(s, n_rows, payload=None):
    riota = lax.broadcasted_iota(jnp.int32, s.shape, 0)
    vals, picks = [], []
    for _ in range(PEER_TOPK):
        m = jnp.max(s, axis=0, keepdims=True)
        am = jnp.min(jnp.where(s == m, riota, n_rows), axis=0, keepdims=True)
        hit = riota == am
        vals.append(m)
        if payload is None:
            picks.append(am)
        else:
            picks.append(jnp.max(jnp.where(hit, payload, -1), axis=0, keepdims=True))
        s = jnp.where(hit, -jnp.inf, s)
    return jnp.concatenate(vals, axis=0), jnp.concatenate(picks, axis=0)


def _out_kernel(x_ref, oa_ref, ob1_ref, l1_ref, ob4_ref, l4_ref, ob16_ref, l16_ref, oc_ref,
                gn_ref, wo_ref, fn_ref, wq_ref, keys_ref,
                x1_ref, hn_ref, idx_ref, gate_ref,
                so4, sl4, so16, sl16, *, tm):
    ng = ob1_ref.shape[1] // LANES
    for r, o_src, l_src, o_dst, l_dst in ((4, ob4_ref, l4_ref, so4, sl4), (16, ob16_ref, l16_ref, so16, sl16)):
        for c in range(r):
            for g in range(ng):
                o_dst[g, pl.ds(c, tm // r, stride=r), :] = o_src[c, :, g * LANES:(g + 1) * LANES]
                l_dst[g, pl.ds(c, tm // r, stride=r), :] = l_src[c, :, g * LANES:(g + 1) * LANES]
    cat = lambda ref: jnp.concatenate([ref[g] for g in range(ng)], axis=-1)
    l1, l4, l16 = l1_ref[...], cat(sl4), cat(sl16)
    mx = jnp.maximum(jnp.maximum(l1, l4), l16)
    e1, e4, e16 = jnp.exp(l1 - mx), jnp.exp(l4 - mx), jnp.exp(l16 - mx)
    ob = (e1 * ob1_ref[...] + e4 * cat(so4) + e16 * cat(so16)) / (e1 + e4 + e16)
    mix = jnp.concatenate([_rms_rows(oa_ref[...]), _rms_rows(ob), _rms_rows(oc_ref[...])], axis=-1) * gn_ref[...]
    x1 = x_ref[...] + jnp.dot(mix.astype(BF16), wo_ref[...], preferred_element_type=F32)
    x1_ref[...] = x1
    hn = _rms_rows(x1) * fn_ref[...]
    hn_ref[...] = hn
    qp = jnp.dot(hn.astype(BF16), wq_ref[...], preferred_element_type=F32)

    idx_rows, gate_rows = [], []
    for h in range(PEER_HEADS):
        qh = qp[:, h * LANES:(h + 1) * LANES].astype(BF16)
        sv, si = [], []
        for p in range(2):
            st = lax.dot_general(keys_ref[h, p], qh, (((1,), (1,)), ((), ())), preferred_element_type=F32)
            v, i = _top_rows(st, N_KEYS)
            sv.append(v)
            si.append(i)
        cand = jnp.concatenate([sv[1] + sv[0][a:a + 1] for a in range(PEER_TOPK)], axis=0)
        cidx = jnp.concatenate([si[1] + si[0][a:a + 1] * N_KEYS for a in range(PEER_TOPK)], axis=0)
        best, eidx = _top_rows(cand, PEER_TOPK * PEER_TOPK, payload=cidx)
        e = jnp.exp(best - best[0:1])
        gate_rows.append(e / jnp.sum(e, axis=0, keepdims=True))
        idx_rows.append(eidx)
    idx_ref[...] = jnp.concatenate(idx_rows, axis=0).T
    gate_ref[...] = jnp.concatenate(gate_rows, axis=0).T


def _out_call(x2d, oa, ob1, l1, ob4, l4, ob16, l16, oc, gn, wo, fn, wq, keys_p, *, batch, seq):
    n, d = x2d.shape
    tm = TOKEN_TILE
    nt = seq // tm
    bw = 3 * LANES
    row = lambda i: (i, 0)
    const2 = lambda i: (0, 0)
    bmap = lambda i: (i // nt, 0, i % nt, 0)
    n_e = PEER_HEADS * PEER_TOPK
    return pl.pallas_call(
        functools.partial(_out_kernel, tm=tm),
        name="out_peerq",
        grid=(n // tm,),
        in_specs=[
            pl.BlockSpec((tm, d), row),
            pl.BlockSpec((tm, bw), row),
            pl.BlockSpec((None, None, tm, bw), bmap), pl.BlockSpec((None, None, tm, bw), bmap),
            pl.BlockSpec((None, 4, tm // 4, bw), bmap), pl.BlockSpec((None, 4, tm // 4, bw), bmap),
            pl.BlockSpec((None, 16, tm // 16, bw), bmap), pl.BlockSpec((None, 16, tm // 16, bw), bmap),
            pl.BlockSpec((tm, 2 * LANES), row),
            pl.BlockSpec(gn.shape, const2),
            pl.BlockSpec(wo.shape, const2),
            pl.BlockSpec(fn.shape, const2),
            pl.BlockSpec(wq.shape, const2),
            pl.BlockSpec(keys_p.shape, lambda i: (0, 0, 0, 0)),
        ],
        out_specs=[pl.BlockSpec((tm, d), row), pl.BlockSpec((tm, d), row),
                   pl.BlockSpec((tm, n_e), row), pl.BlockSpec((tm, n_e), row)],
        out_shape=[jax.ShapeDtypeStruct((n, d), F32), jax.ShapeDtypeStruct((n, d), F32),
                   jax.ShapeDtypeStruct((n, n_e), jnp.int32), jax.ShapeDtypeStruct((n, n_e), F32)],
        scratch_shapes=[pltpu.VMEM((bw // LANES, tm, LANES), F32)] * 4,
        compiler_params=pltpu.CompilerParams(dimension_semantics=("arbitrary",), vmem_limit_bytes=VMEM_LIMIT),
    )(x2d, oa, ob1, l1, ob4, l4, ob16, l16, oc, gn, wo, fn, wq, keys_p)


def _unpack_row(w):
    lo = pltpu.bitcast(w << 16, F32)
    hi = pltpu.bitcast(w & jnp.uint32(0xFFFF0000), F32)
    return lo, hi


def _peer_u_kernel(idx_ref, hn_ref, gate_ref, tbl_ref, act_ref, p_ref, *, nt, ne):
    def token(t, carry):
        x8 = hn_ref[t]
        xlo, xhi = x8[0:4], x8[4:8]
        for k in range(ne):
            lo, hi = _unpack_row(tbl_ref[idx_ref[t, k]])
            p_ref[pl.ds(4 * k, 4), :] = lo * xlo + hi * xhi
        q = (p_ref[pl.ds(0, ne, stride=4), :] + p_ref[pl.ds(1, ne, stride=4), :]
             + p_ref[pl.ds(2, ne, stride=4), :] + p_ref[pl.ds(3, ne, stride=4), :])
        s_col = jnp.sum(q, axis=1, keepdims=True)
        diag = lax.broadcasted_iota(jnp.int32, (ne, ne), 0) == lax.broadcasted_iota(jnp.int32, (ne, ne), 1)
        s_row = jnp.sum(jnp.where(diag, s_col, 0.0), axis=0, keepdims=True)
        gelu = 0.5 * s_row * (1.0 + lax.erf(s_row * (2.0 ** -0.5)))
        act_ref[pl.ds(t, 1), :] = gelu * gate_ref[pl.ds(t, 1), :]
        return carry

    lax.fori_loop(0, nt, token, 0)


def _peer_v_kernel(idx_ref, act_ref, x_ref, tbl_ref, o_ref, *, nt, ne):
    n_acc = 4

    def token(t, carry):
        acc_lo = [jnp.zeros((4, LANES), F32) for _ in range(n_acc)]
        acc_hi = [jnp.zeros((4, LANES), F32) for _ in range(n_acc)]
        for k in range(ne):
            lo, hi = _unpack_row(tbl_ref[idx_ref[t, k]])
            a = act_ref[t, k]
            acc_lo[k % n_acc] = acc_lo[k % n_acc] + a * lo
            acc_hi[k % n_acc] = acc_hi[k % n_acc] + a * hi
        lo = (acc_lo[0] + acc_lo[1]) + (acc_lo[2] + acc_lo[3])
        hi = (acc_hi[0] + acc_hi[1]) + (acc_hi[2] + acc_hi[3])
        o_ref[t] = x_ref[t] + jnp.concatenate([lo, hi], axis=0)
        return carry

    lax.fori_loop(0, nt, token, 0)


def _peer_call(x1, hn, idx, gate, tbl_u, tbl_v):
    n, d = x1.shape
    ne = idx.shape[1]
    nt = PEER_TOKENS
    rows = d // LANES
    row2 = lambda i: (i, 0)
    row3 = lambda i: (i, 0, 0)
    tbl_spec = pl.BlockSpec(tbl_u.shape, lambda i: (0, 0, 0), pipeline_mode=pl.Buffered(1))
    smem_spec = pl.BlockSpec((nt, ne), row2, memory_space=pltpu.SMEM)
    params = pltpu.CompilerParams(dimension_semantics=("arbitrary",), vmem_limit_bytes=PEER_VMEM_LIMIT)
    act = pl.pallas_call(
        functools.partial(_peer_u_kernel, nt=nt, ne=ne),
        name="peer_u",
        grid=(n // nt,),
        in_specs=[smem_spec, pl.BlockSpec((nt, rows, LANES), row3), pl.BlockSpec((nt, ne), row2), tbl_spec],
        out_specs=pl.BlockSpec((nt, ne), row2),
        out_shape=jax.ShapeDtypeStruct((n, ne), F32),
        scratch_shapes=[pltpu.VMEM((4 * ne, LANES), F32)],
        compiler_params=params,
    )(idx, hn.reshape(n, rows, LANES), gate, tbl_u)
    out = pl.pallas_call(
        functools.partial(_peer_v_kernel, nt=nt, ne=ne),
        name="peer_v",
        grid=(n // nt,),
        in_specs=[smem_spec, smem_spec, pl.BlockSpec((nt, rows, LANES), row3), tbl_spec],
        out_specs=pl.BlockSpec((nt, rows, LANES), row3),
        out_shape=jax.ShapeDtypeStruct((n, rows, LANES), F32),
        compiler_params=params,
    )(idx, act, x1.reshape(n, rows, LANES), tbl_v)
    return out.reshape(n, d)


def _in_proj_perm():
    widths = (A_HEADS, A_KV_HEADS, A_KV_HEADS, B_HEADS, B_HEADS, B_HEADS, C_HEADS, C_KV_HEADS, C_KV_HEADS)
    orders = (A_Q_ORDER, None, None, None, None, None, C_Q_ORDER, None, None)
    cols, base = [], 0
    for w, order in zip(widths, orders):
        for hd in (order if order is not None else range(w)):
            cols.extend(range(base + hd * HEAD_DIM, base + (hd + 1) * HEAD_DIM))
        base += w * HEAD_DIM
    return np.asarray(cols, dtype=np.int32)


def _mix_perm():
    cols = []
    for hd in A_Q_ORDER:
        cols.extend(range(hd * HEAD_DIM, (hd + 1) * HEAD_DIM))
    base = A_HEADS * HEAD_DIM
    cols.extend(range(base, base + B_HEADS * HEAD_DIM))
    base += B_HEADS * HEAD_DIM
    for hd in C_Q_ORDER:
        cols.extend(range(base + hd * HEAD_DIM, base + (hd + 1) * HEAD_DIM))
    return np.asarray(cols, dtype=np.int32)


def _rope_tables(seq):
    pos = jnp.arange(seq, dtype=F32)
    f64 = ROPE_THETA ** (-jnp.arange(0, HEAD_DIM, 2, dtype=F32) / HEAD_DIM)
    ang = pos[:, None] * f64[None, :]
    cos1 = jnp.concatenate([jnp.cos(ang), jnp.cos(ang)], axis=-1)
    sin1 = jnp.concatenate([-jnp.sin(ang), jnp.sin(ang)], axis=-1)
    half = HEAD_DIM // 2
    f32_ = ROPE_THETA ** (-jnp.arange(0, half, 2, dtype=F32) / half)
    rows = seq // GRID_W
    row = jnp.repeat(jnp.arange(rows, dtype=F32), GRID_W)
    col = jnp.tile(jnp.arange(GRID_W, dtype=F32), rows)
    ar, ac = row[:, None] * f32_[None, :], col[:, None] * f32_[None, :]
    cos_a = jnp.concatenate([jnp.cos(ar), jnp.cos(ar), jnp.cos(ac), jnp.cos(ac)], axis=-1)
    sin_a = jnp.concatenate([-jnp.sin(ar), jnp.sin(ar), -jnp.sin(ac), jnp.sin(ac)], axis=-1)
    two = lambda t: jnp.concatenate([t, t], axis=-1)
    return jnp.stack([two(cos_a), two(cos1)]), jnp.stack([two(sin_a), two(sin1)])


def _group_gains(g):
    rows = []
    mix_id = {"a": 0, "b": 1, "c": 2}
    for mixer, role, kind in _GROUPS:
        if kind is None:
            rows.append(jnp.ones((LANES,), F32))
        else:
            gg = g[mix_id[mixer], 0 if role == "q" else 1].astype(F32)
            if role == "q":
                gg = gg * ATTN_SCALE
            rows.append(jnp.concatenate([gg, gg]))
    return jnp.stack(rows)


def _pack_table(t):
    e, d = t.shape
    b = lax.bitcast_convert_type(t.astype(BF16), jnp.uint16).astype(jnp.uint32)
    w = (b[:, d // 2:] << 16) | b[:, :d // 2]
    return w.reshape(e, d // 2 // LANES, LANES)


def _pad_keys(keys):
    z = jnp.zeros_like(keys[:, 0])
    k0 = jnp.concatenate([keys[:, 0], z], axis=-1)
    k1 = jnp.concatenate([z, keys[:, 1]], axis=-1)
    return jnp.stack([k0, k1], axis=1).astype(BF16)


def _layer(x2d, batch, seq, g_attn, w_in, gains, cos_t, sin_t, bd, sink, g_out, w_out, g_ffn, wq, keys_p,
           tbl_u, tbl_v):
    n = x2d.shape[0]
    aq, ak, av, b1, b4, b16, cq, ck, cv = _proj_call(x2d, g_attn, w_in, bd, gains, cos_t, sin_t,
                                                     batch=batch, seq=seq)
    oa = _attn_call(
        "attn_a", aq, ak, av, (batch,),
        pl.BlockSpec((seq, 3 * LANES), lambda b: (b, 0)),
        pl.BlockSpec((seq, LANES), lambda b: (b, 0)),
        pl.BlockSpec((seq, LANES), lambda b: (b, 0)),
        jax.ShapeDtypeStruct((n, 3 * LANES), F32), pl.BlockSpec((seq, 3 * LANES), lambda b: (b, 0)),
        seq_len=seq, tq=256, win=seq, half=None, n_qp=3, n_kp=1)[0]
    ob, lse = [], []
    for (window, r), arr in zip(DILATIONS, (b1, b4, b16)):
        length = seq // r
        half = window // (2 * r)
        tq = min(128, length)
        win = min(tq + 2 * half, length)
        spec = lambda c: pl.BlockSpec((None, None, length, 3 * LANES), lambda b, s, c=c: (b, s, 0, c))
        o_r, l_r = _attn_call(
            f"attn_b{r}", arr, arr, arr, (batch, r), spec(0), spec(1), spec(2),
            jax.ShapeDtypeStruct((batch, r, length, 3 * LANES), F32), spec(0),
            seq_len=length, tq=tq, win=win, half=half, n_qp=3, n_kp=3, emit_lse=True)
        ob.append(o_r)
        lse.append(l_r)
    tq = LOCAL_HALF_WINDOW
    oc = _attn_call(
        "attn_c", cq, ck, cv, (batch,),
        pl.BlockSpec((seq, 2 * LANES), lambda b: (b, 0)),
        pl.BlockSpec((seq, LANES), lambda b: (b, 0)),
        pl.BlockSpec((seq, LANES), lambda b: (b, 0)),
        jax.ShapeDtypeStruct((n, 2 * LANES), F32), pl.BlockSpec((seq, 2 * LANES), lambda b: (b, 0)),
        seq_len=seq, tq=tq, win=min(tq + 2 * LOCAL_HALF_WINDOW, seq), half=LOCAL_HALF_WINDOW,
        n_qp=2, n_kp=1, sink=sink)[0]
    x1, hn, idx, gate = _out_call(x2d, oa, ob[0], lse[0], ob[1], lse[1], ob[2], lse[2], oc,
                                  g_out, w_out, g_ffn, wq, keys_p, batch=batch, seq=seq)
    return _peer_call(x1, hn, idx, gate, tbl_u, tbl_v)


def kernel(x, attn_norm, w_in, qk_gain, sink_logits, out_norm, w_out, ffn_norm, peer_wq, peer_keys, peer_u, peer_v):
    batch, seq, d = x.shape
    depth = w_in.shape[0]
    in_perm = _in_proj_perm()
    mix_perm = _mix_perm()
    cos_t, sin_t = _rope_tables(seq)
    bd = jnp.asarray(np.kron(np.eye(2), np.ones((HEAD_DIM, HEAD_DIM))), dtype=BF16)
    sink_perm = np.asarray([C_Q_ORDER[2 * j + hf] for j in range(2) for hf in range(2)], dtype=np.int32)
    x2d = x.reshape(batch * seq, d)
    for layer in range(depth):
        x2d = _layer(
            x2d, batch, seq,
            attn_norm[layer][None, :].astype(F32),
            w_in[layer][:, in_perm].astype(BF16),
            _group_gains(qk_gain[layer]),
            cos_t, sin_t, bd,
            sink_logits[layer][sink_perm].astype(F32),
            out_norm[layer][mix_perm][None, :].astype(F32),
            w_out[layer][mix_perm, :].astype(BF16),
            ffn_norm[layer][None, :].astype(F32),
            peer_wq[layer].astype(BF16),
            _pad_keys(peer_keys[layer]),
            _pack_table(peer_u[layer]),
            _pack_table(peer_v[layer]),
        )
    return x2d.reshape(batch, seq, d)
```

```python
import functools

import numpy as np
import jax
import jax.numpy as jnp
from jax import lax
from jax.experimental import pallas as pl
from jax.experimental.pallas import tpu as pltpu

F32 = jnp.float32
BF16 = jnp.bfloat16

LANES = 128
HEAD_DIM = 64
A_HEADS, A_KV_HEADS = 6, 2
B_HEADS = 6
C_HEADS, C_KV_HEADS = 4, 2
ROPE_THETA = 10000.0
GRID_W = 64
DILATIONS = ((128, 1), (512, 4), (2048, 16))
LOCAL_HALF_WINDOW = 128
PEER_HEADS = 8
N_KEYS = 128
PEER_TOPK = 16
NORM_EPS = 1e-6
NEG_INF = -1e30
ATTN_SCALE = HEAD_DIM ** -0.5

TOKEN_TILE = 256
PEER_TOKENS = 32
VMEM_LIMIT = 48 * 1024 * 1024
PEER_VMEM_LIMIT = 56 * 1024 * 1024

_GROUPS = ([("a", "q", "axial")] * 3 + [("a", "k", "axial")] + [("a", "v", None)]
           + [("b", "q", "1d")] * 3 + [("b", "k", "1d")] * 3 + [("b", "v", None)] * 3
           + [("c", "q", "1d")] * 2 + [("c", "k", "1d")] + [("c", "v", None)])
N_GROUPS = len(_GROUPS)
A_Q_ORDER = (0, 3, 1, 4, 2, 5)
C_Q_ORDER = (0, 2, 1, 3)


def _rms_rows(t):
    return t * lax.rsqrt(jnp.mean(t * t, axis=-1, keepdims=True) + NORM_EPS)


def _proj_kernel(x_ref, g_ref, w_ref, bd_ref, gain_ref, cos_ref, sin_ref,
                 aq_ref, ak_ref, av_ref, b1_ref, b4_ref, b16_ref, cq_ref, ck_ref, cv_ref,
                 scr_ref, *, tm):
    x = x_ref[...]
    h = _rms_rows(x) * g_ref[...]
    proj = jnp.dot(h.astype(BF16), w_ref[...], preferred_element_type=F32)
    bd = bd_ref[...]
    lane = lax.broadcasted_iota(jnp.int32, (1, LANES), 1)

    def norm_rope(y, j, kind):
        sq = y * y
        hi = sq.astype(BF16)
        lo = (sq - hi.astype(F32)).astype(BF16)
        ssq = (jnp.dot(hi, bd, preferred_element_type=F32)
               + jnp.dot(lo, bd, preferred_element_type=F32))
        yn = y * lax.rsqrt(ssq * (1.0 / HEAD_DIM) + NORM_EPS) * gain_ref[j:j + 1, :]
        t, sh = (0, 16) if kind == "axial" else (1, 32)
        first = (lane % (2 * sh)) < sh
        partner = jnp.where(first, pltpu.roll(yn, LANES - sh, 1), pltpu.roll(yn, sh, 1))
        return yn * cos_ref[t] + partner * sin_ref[t]

    outs = {("a", "q"): aq_ref, ("a", "k"): ak_ref, ("a", "v"): av_ref,
            ("c", "q"): cq_ref, ("c", "k"): ck_ref, ("c", "v"): cv_ref}
    pos = {}
    b_col = 0
    for j, (mixer, role, kind) in enumerate(_GROUPS):
        y = proj[:, j * LANES:(j + 1) * LANES]
        if kind is not None:
            y = norm_rope(y, j, kind)
        if mixer == "b":
            scr_ref[b_col] = y
            b1_ref[:, b_col * LANES:(b_col + 1) * LANES] = y.astype(BF16)
            b_col += 1
        else:
            c = pos.get((mixer, role), 0)
            outs[(mixer, role)][:, c * LANES:(c + 1) * LANES] = y.astype(BF16)
            pos[(mixer, role)] = c + 1
    for r, ref in ((4, b4_ref), (16, b16_ref)):
        for c in range(r):
            for g in range(b_col):
                ref[c, :, g * LANES:(g + 1) * LANES] = scr_ref[g, pl.ds(c, tm // r, stride=r), :].astype(BF16)


def _proj_call(x2d, g, w, bd, gains, cos_t, sin_t, *, batch, seq):
    n = x2d.shape[0]
    tm = TOKEN_TILE
    nt = seq // tm
    bw = 9 * LANES
    row = lambda i: (i, 0)
    const2 = lambda i: (0, 0)
    bmap = lambda i: (i // nt, 0, i % nt, 0)
    outs = [
        (jax.ShapeDtypeStruct((n, 3 * LANES), BF16), pl.BlockSpec((tm, 3 * LANES), row)),
        (jax.ShapeDtypeStruct((n, LANES), BF16), pl.BlockSpec((tm, LANES), row)),
        (jax.ShapeDtypeStruct((n, LANES), BF16), pl.BlockSpec((tm, LANES), row)),
        (jax.ShapeDtypeStruct((batch, 1, seq, bw), BF16), pl.BlockSpec((None, None, tm, bw), bmap)),
        (jax.ShapeDtypeStruct((batch, 4, seq // 4, bw), BF16), pl.BlockSpec((None, 4, tm // 4, bw), bmap)),
        (jax.ShapeDtypeStruct((batch, 16, seq // 16, bw), BF16), pl.BlockSpec((None, 16, tm // 16, bw), bmap)),
        (jax.ShapeDtypeStruct((n, 2 * LANES), BF16), pl.BlockSpec((tm, 2 * LANES), row)),
        (jax.ShapeDtypeStruct((n, LANES), BF16), pl.BlockSpec((tm, LANES), row)),
        (jax.ShapeDtypeStruct((n, LANES), BF16), pl.BlockSpec((tm, LANES), row)),
    ]
    return pl.pallas_call(
        functools.partial(_proj_kernel, tm=tm),
        name="proj",
        grid=(n // tm,),
        in_specs=[
            pl.BlockSpec((tm, x2d.shape[1]), row),
            pl.BlockSpec(g.shape, const2),
            pl.BlockSpec(w.shape, const2),
            pl.BlockSpec(bd.shape, const2),
            pl.BlockSpec(gains.shape, const2),
            pl.BlockSpec((2, tm, LANES), lambda i: (0, i % nt, 0)),
            pl.BlockSpec((2, tm, LANES), lambda i: (0, i % nt, 0)),
        ],
        out_specs=[o[1] for o in outs],
        out_shape=[o[0] for o in outs],
        scratch_shapes=[pltpu.VMEM((bw // LANES, tm, LANES), F32)],
        compiler_params=pltpu.CompilerParams(dimension_semantics=("arbitrary",), vmem_limit_bytes=VMEM_LIMIT),
    )(x2d, g, w, bd, gains, cos_t, sin_t)


def _attn_kernel(*refs, seq_len, tq, win, half, n_qp, n_kp, has_sink, emit_lse):
    it = iter(refs)
    q_ref, k_ref, v_ref = next(it), next(it), next(it)
    sink_ref = next(it) if has_sink else None
    o_ref = next(it)
    lse_ref = next(it) if emit_lse else None
    lane = lax.broadcasted_iota(jnp.int32, (1, LANES), 1)
    low = lane < HEAD_DIM

    def tile(i, carry):
        r0 = pl.multiple_of(i * tq, tq)
        if half is None or win == seq_len:
            ks = 0
        else:
            ks = pl.multiple_of(jnp.clip(r0 - half, 0, seq_len - win), half)
        if half is not None:
            d = (r0 - ks) + lax.broadcasted_iota(jnp.int32, (tq, win), 0) - lax.broadcasted_iota(jnp.int32, (tq, win), 1)
            visible = jnp.abs(d) <= half
        for j in range(n_qp):
            kj = j if n_kp > 1 else 0
            q = q_ref[pl.ds(r0, tq), j * LANES:(j + 1) * LANES]
            k = k_ref[pl.ds(ks, win), kj * LANES:(kj + 1) * LANES]
            v = v_ref[pl.ds(ks, win), kj * LANES:(kj + 1) * LANES]
            o_h, lse_h = [], []
            for hf in range(2):
                qm = jnp.where(low if hf == 0 else jnp.logical_not(low), q, jnp.zeros_like(q))
                s = lax.dot_general(qm, k, (((1,), (1,)), ((), ())), preferred_element_type=F32)
                if half is not None:
                    s = jnp.where(visible, s, NEG_INF)
                m = jnp.max(s, axis=-1, keepdims=True)
                if has_sink:
                    sink = sink_ref[2 * j + hf]
                    m = jnp.maximum(m, sink)
                p = jnp.exp(s - m)
                l = jnp.sum(p, axis=-1, keepdims=True)
                if has_sink:
                    l = l + jnp.exp(sink - m)
                o = jnp.dot(p.astype(BF16), v, preferred_element_type=F32)
                o_h.append(o / l)
                if emit_lse:
                    lse_h.append(m + jnp.log(l))
            o_ref[pl.ds(r0, tq), j * LANES:(j + 1) * LANES] = jnp.where(low, o_h[0], o_h[1])
            if emit_lse:
                lse_ref[pl.ds(r0, tq), j * LANES:(j + 1) * LANES] = jnp.where(low, lse_h[0], lse_h[1])
        return carry

    lax.fori_loop(0, seq_len // tq, tile, 0)


def _attn_call(name, q_arg, k_arg, v_arg, grid, q_spec, k_spec, v_spec, out_shape, out_spec, *,
               seq_len, tq, win, half, n_qp, n_kp, sink=None, emit_lse=False):
    in_specs = [q_spec, k_spec, v_spec]
    args = [q_arg, k_arg, v_arg]
    if sink is not None:
        in_specs.append(pl.BlockSpec(memory_space=pltpu.SMEM))
        args.append(sink)
    n_out = 2 if emit_lse else 1
    res = pl.pallas_call(
        functools.partial(_attn_kernel, seq_len=seq_len, tq=tq, win=win, half=half, n_qp=n_qp, n_kp=n_kp,
                          has_sink=sink is not None, emit_lse=emit_lse),
        name=name,
        grid=grid,
        in_specs=in_specs,
        out_specs=[out_spec] * n_out,
        out_shape=[out_shape] * n_out,
        compiler_params=pltpu.CompilerParams(dimension_semantics=("arbitrary",) * len(grid),
                                             vmem_limit_bytes=VMEM_LIMIT),
    )(*args)
    return res


def _top_rows(s, payload=None):
    n_rows = s.shape[0]
    riota = lax.broadcasted_iota(jnp.int32, s.shape, 0).astype(F32)
    vals, picks = [], []
    for _ in range(PEER_TOPK):
        m = jnp.max(s, axis=0, keepdims=True)
        am = jnp.min(jnp.where(s == m, riota, float(n_rows)), axis=0, keepdims=True)
        hit = riota == am
        vals.append(m)
        if payload is None:
            picks.append(am)
        else:
            picks.append(jnp.max(jnp.where(hit, payload, -1.0), axis=0, keepdims=True))
        s = jnp.where(hit, -jnp.inf, s)
    return jnp.concatenate(vals, axis=0), jnp.concatenate(picks, axis=0)


def _pair_candidates(t0, t1, combine):
    h = PEER_TOPK // 2
    blocks = [combine(t0[0:1], t1)]
    blocks += [combine(t0[a:a + 1], t1[0:h]) for a in range(1, h)]
    blocks.append(combine(t0[h:], t1[0:1]))
    return jnp.concatenate(blocks, axis=0)


def _out_kernel(x_ref, oa_ref, ob1_ref, l1_ref, ob4_ref, l4_ref, ob16_ref, l16_ref, oc_ref,
                gn_ref, wo_ref, fn_ref, wq_ref, keys_ref,
                x1_ref, hn_ref, idx_ref, gate_ref,
                so4, sl4, so16, sl16, *, tm):
    ng = ob1_ref.shape[1] // LANES
    for r, o_src, l_src, o_dst, l_dst in ((4, ob4_ref, l4_ref, so4, sl4), (16, ob16_ref, l16_ref, so16, sl16)):
        for c in range(r):
            for g in range(ng):
                o_dst[g, pl.ds(c, tm // r, stride=r), :] = o_src[c, :, g * LANES:(g + 1) * LANES]
                l_dst[g, pl.ds(c, tm // r, stride=r), :] = l_src[c, :, g * LANES:(g + 1) * LANES]
    cat = lambda ref: jnp.concatenate([ref[g] for g in range(ng)], axis=-1)
    l1, l4, l16 = l1_ref[...], cat(sl4), cat(sl16)
    mx = jnp.maximum(jnp.maximum(l1, l4), l16)
    e1, e4, e16 = jnp.exp(l1 - mx), jnp.exp(l4 - mx), jnp.exp(l16 - mx)
    ob = (e1 * ob1_ref[...] + e4 * cat(so4) + e16 * cat(so16)) / (e1 + e4 + e16)
    mix = jnp.concatenate([_rms_rows(oa_ref[...]), _rms_rows(ob), _rms_rows(oc_ref[...])], axis=-1) * gn_ref[...]
    x1 = x_ref[...] + jnp.dot(mix.astype(BF16), wo_ref[...], preferred_element_type=F32)
    x1_ref[...] = x1
    hn = _rms_rows(x1) * fn_ref[...]
    hn_ref[...] = hn
    qp = jnp.dot(hn.astype(BF16), wq_ref[...], preferred_element_type=F32)

    idx_rows, gate_rows = [], []
    for h in range(PEER_HEADS):
        qh = qp[:, h * LANES:(h + 1) * LANES].astype(BF16)
        sv, si = [], []
        for p in range(2):
            st = lax.dot_general(keys_ref[h, p], qh, (((1,), (1,)), ((), ())), preferred_element_type=F32)
            v, i = _top_rows(st)
            sv.append(v)
            si.append(i)
        cand = _pair_candidates(sv[0], sv[1], lambda a, b: a + b)
        cidx = _pair_candidates(si[0], si[1], lambda a, b: a * float(N_KEYS) + b)
        best, eidx = _top_rows(cand, payload=cidx)
        e = jnp.exp(best - best[0:1])
        gate_rows.append(e / jnp.sum(e, axis=0, keepdims=True))
        idx_rows.append(eidx)
    idx_ref[...] = jnp.concatenate(idx_rows, axis=0).T.astype(jnp.int32)
    gate_ref[...] = jnp.concatenate(gate_rows, axis=0).T


def _out_call(x2d, oa, ob1, l1, ob4, l4, ob16, l16, oc, gn, wo, fn, wq, keys_p, *, batch, seq):
    n, d = x2d.shape
    tm = TOKEN_TILE
    nt = seq // tm
    bw = 3 * LANES
    row = lambda i: (i, 0)
    const2 = lambda i: (0, 0)
    bmap = lambda i: (i // nt, 0, i % nt, 0)
    n_e = PEER_HEADS * PEER_TOPK
    return pl.pallas_call(
        functools.partial(_out_kernel, tm=tm),
        name="out_peerq",
        grid=(n // tm,),
        in_specs=[
            pl.BlockSpec((tm, d), row),
            pl.BlockSpec((tm, bw), row),
            pl.BlockSpec((None, None, tm, bw), bmap), pl.BlockSpec((None, None, tm, bw), bmap),
            pl.BlockSpec((None, 4, tm // 4, bw), bmap), pl.BlockSpec((None, 4, tm // 4, bw), bmap),
            pl.BlockSpec((None, 16, tm // 16, bw), bmap), pl.BlockSpec((None, 16, tm // 16, bw), bmap),
            pl.BlockSpec((tm, 2 * LANES), row),
            pl.BlockSpec(gn.shape, const2),
            pl.BlockSpec(wo.shape, const2),
            pl.BlockSpec(fn.shape, const2),
            pl.BlockSpec(wq.shape, const2),
            pl.BlockSpec(keys_p.shape, lambda i: (0, 0, 0, 0)),
        ],
        out_specs=[pl.BlockSpec((tm, d), row), pl.BlockSpec((tm, d), row),
                   pl.BlockSpec((tm, n_e), row), pl.BlockSpec((tm, n_e), row)],
        out_shape=[jax.ShapeDtypeStruct((n, d), F32), jax.ShapeDtypeStruct((n, d), F32),
                   jax.ShapeDtypeStruct((n, n_e), jnp.int32), jax.ShapeDtypeStruct((n, n_e), F32)],
        scratch_shapes=[pltpu.VMEM((bw // LANES, tm, LANES), F32)] * 4,
        compiler_params=pltpu.CompilerParams(dimension_semantics=("arbitrary",), vmem_limit_bytes=VMEM_LIMIT),
    )(x2d, oa, ob1, l1, ob4, l4, ob16, l16, oc, gn, wo, fn, wq, keys_p)


def _unpack_row(w):
    lo = pltpu.bitcast(w << 16, F32)
    hi = pltpu.bitcast(w & jnp.uint32(0xFFFF0000), F32)
    return lo, hi


def _peer_u_kernel(idx_ref, hn_ref, gate_ref, tbl_ref, act_ref, p_ref, *, nt, ne):
    diag = lax.broadcasted_iota(jnp.int32, (ne, ne), 0) == lax.broadcasted_iota(jnp.int32, (ne, ne), 1)
    for t in range(nt):
        x8 = hn_ref[t]
        xlo, xhi = x8[0:4], x8[4:8]
        for k in range(ne):
            lo, hi = _unpack_row(tbl_ref[idx_ref[t, k]])
            p_ref[t, pl.ds(4 * k, 4), :] = lo * xlo + hi * xhi
    for t in range(nt):
        q = (p_ref[t, pl.ds(0, ne, stride=4), :] + p_ref[t, pl.ds(1, ne, stride=4), :]
             + p_ref[t, pl.ds(2, ne, stride=4), :] + p_ref[t, pl.ds(3, ne, stride=4), :])
        s_col = jnp.sum(q, axis=1, keepdims=True)
        s_row = jnp.sum(jnp.where(diag, s_col, 0.0), axis=0, keepdims=True)
        gelu = 0.5 * s_row * (1.0 + lax.erf(s_row * (2.0 ** -0.5)))
        act_ref[pl.ds(t, 1), :] = gelu * gate_ref[pl.ds(t, 1), :]


def _peer_v_kernel(idx_ref, act_ref, x_ref, tbl_ref, o_ref, arep_ref, *, nt, ne):
    for t in range(nt):
        arep_ref[t] = jnp.broadcast_to(act_ref[pl.ds(t, 1), :], (ne, ne)).T
    for t in range(nt):
        acc = [jnp.zeros((4, LANES), F32) for _ in range(4)]
        for k in range(ne):
            lo, hi = _unpack_row(tbl_ref[idx_ref[t, k]])
            a = arep_ref[t, pl.ds(k, 1), :]
            j = 2 * (k % 2)
            acc[j], acc[j + 1] = acc[j] + a * lo, acc[j + 1] + a * hi
        o_ref[t] = x_ref[t] + jnp.concatenate([acc[0] + acc[2], acc[1] + acc[3]], axis=0)


def _peer_call(x1, hn, idx, gate, tbl_u, tbl_v):
    n, d = x1.shape
    ne = idx.shape[1]
    nt = PEER_TOKENS
    rows = d // LANES
    row2 = lambda i: (i, 0)
    row3 = lambda i: (i, 0, 0)
    tbl_spec = pl.BlockSpec(tbl_u.shape, lambda i: (0, 0, 0), pipeline_mode=pl.Buffered(1))
    smem_spec = pl.BlockSpec((nt, ne), row2, memory_space=pltpu.SMEM, pipeline_mode=pl.Buffered(1))
    params = pltpu.CompilerParams(dimension_semantics=("arbitrary",), vmem_limit_bytes=PEER_VMEM_LIMIT)
    act = pl.pallas_call(
        functools.partial(_peer_u_kernel, nt=nt, ne=ne),
        name="peer_u",
        grid=(n // nt,),
        in_specs=[smem_spec, pl.BlockSpec((nt, rows, LANES), row3), pl.BlockSpec((nt, ne), row2), tbl_spec],
        out_specs=pl.BlockSpec((nt, ne), row2),
        out_shape=jax.ShapeDtypeStruct((n, ne), F32),
        scratch_shapes=[pltpu.VMEM((nt, 4 * ne, LANES), F32)],
        compiler_params=params,
    )(idx, hn.reshape(n, rows, LANES), gate, tbl_u)
    out = pl.pallas_call(
        functools.partial(_peer_v_kernel, nt=nt, ne=ne),
        name="peer_v",
        grid=(n // nt,),
        in_specs=[smem_spec, pl.BlockSpec((nt, ne), row2), pl.BlockSpec((nt, rows, LANES), row3), tbl_spec],
        out_specs=pl.BlockSpec((nt, rows, LANES), row3),
        out_shape=jax.ShapeDtypeStruct((n, rows, LANES), F32),
        scratch_shapes=[pltpu.VMEM((nt, ne, ne), F32)],
        compiler_params=params,
    )(idx, act, x1.reshape(n, rows, LANES), tbl_v)
    return out.reshape(n, d)


def _in_proj_perm():
    widths = (A_HEADS, A_KV_HEADS, A_KV_HEADS, B_HEADS, B_HEADS, B_HEADS, C_HEADS, C_KV_HEADS, C_KV_HEADS)
    orders = (A_Q_ORDER, None, None, None, None, None, C_Q_ORDER, None, None)
    cols, base = [], 0
    for w, order in zip(widths, orders):
        for hd in (order if order is not None else range(w)):
            cols.extend(range(base + hd * HEAD_DIM, base + (hd + 1) * HEAD_DIM))
        base += w * HEAD_DIM
    return np.asarray(cols, dtype=np.int32)


def _mix_perm():
    cols = []
    for hd in A_Q_ORDER:
        cols.extend(range(hd * HEAD_DIM, (hd + 1) * HEAD_DIM))
    base = A_HEADS * HEAD_DIM
    cols.extend(range(base, base + B_HEADS * HEAD_DIM))
    base += B_HEADS * HEAD_DIM
    for hd in C_Q_ORDER:
        cols.extend(range(base + hd * HEAD_DIM, base + (hd + 1) * HEAD_DIM))
    return np.asarray(cols, dtype=np.int32)


def _rope_tables(seq):
    pos = jnp.arange(seq, dtype=F32)
    f64 = ROPE_THETA ** (-jnp.arange(0, HEAD_DIM, 2, dtype=F32) / HEAD_DIM)
    ang = pos[:, None] * f64[None, :]
    cos1 = jnp.concatenate([jnp.cos(ang), jnp.cos(ang)], axis=-1)
    sin1 = jnp.concatenate([-jnp.sin(ang), jnp.sin(ang)], axis=-1)
    half = HEAD_DIM // 2
    f32_ = ROPE_THETA ** (-jnp.arange(0, half, 2, dtype=F32) / half)
    rows = seq // GRID_W
    row = jnp.repeat(jnp.arange(rows, dtype=F32), GRID_W)
    col = jnp.tile(jnp.arange(GRID_W, dtype=F32), rows)
    ar, ac = row[:, None] * f32_[None, :], col[:, None] * f32_[None, :]
    cos_a = jnp.concatenate([jnp.cos(ar), jnp.cos(ar), jnp.cos(ac), jnp.cos(ac)], axis=-1)
    sin_a = jnp.concatenate([-jnp.sin(ar), jnp.sin(ar), -jnp.sin(ac), jnp.sin(ac)], axis=-1)
    two = lambda t: jnp.concatenate([t, t], axis=-1)
    return jnp.stack([two(cos_a), two(cos1)]), jnp.stack([two(sin_a), two(sin1)])


def _group_gains(g):
    rows = []
    mix_id = {"a": 0, "b": 1, "c": 2}
    for mixer, role, kind in _GROUPS:
        if kind is None:
            rows.append(jnp.ones((LANES,), F32))
        else:
            gg = g[mix_id[mixer], 0 if role == "q" else 1].astype(F32)
            if role == "q":
                gg = gg * ATTN_SCALE
            rows.append(jnp.concatenate([gg, gg]))
    return jnp.stack(rows)


def _pack_table(t):
    e, d = t.shape
    b = lax.bitcast_convert_type(t.astype(BF16), jnp.uint16).astype(jnp.uint32)
    w = (b[:, d // 2:] << 16) | b[:, :d // 2]
    return w.reshape(e, d // 2 // LANES, LANES)


def _pad_keys(keys):
    z = jnp.zeros_like(keys[:, 0])
    k0 = jnp.concatenate([keys[:, 0], z], axis=-1)
    k1 = jnp.concatenate([z, keys[:, 1]], axis=-1)
    return jnp.stack([k0, k1], axis=1).astype(BF16)


def _layer(x2d, batch, seq, g_attn, w_in, gains, cos_t, sin_t, bd, sink, g_out, w_out, g_ffn, wq, keys_p,
           tbl_u, tbl_v):
    n = x2d.shape[0]
    aq, ak, av, b1, b4, b16, cq, ck, cv = _proj_call(x2d, g_attn, w_in, bd, gains, cos_t, sin_t,
                                                     batch=batch, seq=seq)
    oa = _attn_call(
        "attn_a", aq, ak, av, (batch,),
        pl.BlockSpec((seq, 3 * LANES), lambda b: (b, 0)),
        pl.BlockSpec((seq, LANES), lambda b: (b, 0)),
        pl.BlockSpec((seq, LANES), lambda b: (b, 0)),
        jax.ShapeDtypeStruct((n, 3 * LANES), F32), pl.BlockSpec((seq, 3 * LANES), lambda b: (b, 0)),
        seq_len=seq, tq=256, win=seq, half=None, n_qp=3, n_kp=1)[0]
    ob, lse = [], []
    for (window, r), arr in zip(DILATIONS, (b1, b4, b16)):
        length = seq // r
        half = window // (2 * r)
        tq = min(128, length)
        win = min(tq + 2 * half, length)
        spec = lambda c: pl.BlockSpec((None, None, length, 3 * LANES), lambda b, s, c=c: (b, s, 0, c))
        o_r, l_r = _attn_call(
            f"attn_b{r}", arr, arr, arr, (batch, r), spec(0), spec(1), spec(2),
            jax.ShapeDtypeStruct((batch, r, length, 3 * LANES), F32), spec(0),
            seq_len=length, tq=tq, win=win, half=half, n_qp=3, n_kp=3, emit_lse=True)
        ob.append(o_r)
        lse.append(l_r)
    tq = LOCAL_HALF_WINDOW
    oc = _attn_call(
        "attn_c", cq, ck, cv, (batch,),
        pl.BlockSpec((seq, 2 * LANES), lambda b: (b, 0)),
        pl.BlockSpec((seq, LANES), lambda b: (b, 0)),
        pl.BlockSpec((seq, LANES), lambda b: (b, 0)),
        jax.ShapeDtypeStruct((n, 2 * LANES), F32), pl.BlockSpec((seq, 2 * LANES), lambda b: (b, 0)),
        seq_len=seq, tq=tq, win=min(tq + 2 * LOCAL_HALF_WINDOW, seq), half=LOCAL_HALF_WINDOW,
        n_qp=2, n_kp=1, sink=sink)[0]
    x1, hn, idx, gate = _out_call(x2d, oa, ob[0], lse[0], ob[1], lse[1], ob[2], lse[2], oc,
                                  g_out, w_out, g_ffn, wq, keys_p, batch=batch, seq=seq)
    return _peer_call(x1, hn, idx, gate, tbl_u, tbl_v)


def kernel(x, attn_norm, w_in, qk_gain, sink_logits, out_norm, w_out, ffn_norm, peer_wq, peer_keys, peer_u, peer_v):
    batch, seq, d = x.shape
    depth = w_in.shape[0]
    in_perm = _in_proj_perm()
    mix_perm = _mix_perm()
    cos_t, sin_t = _rope_tables(seq)
    bd = jnp.asarray(np.kron(np.eye(2), np.ones((HEAD_DIM, HEAD_DIM))), dtype=BF16)
    sink_perm = np.asarray([C_Q_ORDER[2 * j + hf] for j in range(2) for hf in range(2)], dtype=np.int32)
    x2d = x.reshape(batch * seq, d)
    for layer in range(depth):
        x2d = _layer(
            x2d, batch, seq,
            attn_norm[layer][None, :].astype(F32),
            w_in[layer][:, in_perm].astype(BF16),
            _group_gains(qk_gain[layer]),
            cos_t, sin_t, bd,
            sink_logits[layer][sink_perm].astype(F32),
            out_norm[layer][mix_perm][None, :].astype(F32),
            w_out[layer][mix_perm, :].astype(BF16),
            ffn_norm[layer][None, :].astype(F32),
            peer_wq[layer].astype(BF16),
            _pad_keys(peer_keys[layer]),
            _pack_table(peer_u[layer]),
            _pack_table(peer_v[layer]),
        )
    return x2d.reshape(batch, seq, d)
```

```python
import functools

import numpy as np
import jax
import jax.numpy as jnp
from jax import lax
from jax.experimental import pallas as pl
from jax.experimental.pallas import tpu as pltpu

F32 = jnp.float32
BF16 = jnp.bfloat16

LANES = 128
HEAD_DIM = 64
A_HEADS, A_KV_HEADS = 6, 2
B_HEADS = 6
C_HEADS, C_KV_HEADS = 4, 2
ROPE_THETA = 10000.0
GRID_W = 64
DILATIONS = ((128, 1), (512, 4), (2048, 16))
LOCAL_HALF_WINDOW = 128
PEER_HEADS = 8
N_KEYS = 128
PEER_TOPK = 16
NORM_EPS = 1e-6
NEG_INF = -1e30
ATTN_SCALE = HEAD_DIM ** -0.5

TOKEN_TILE = 256
PEER_TOKENS = 32
VMEM_LIMIT = 48 * 1024 * 1024
PEER_VMEM_LIMIT = 56 * 1024 * 1024

_GROUPS = ([("a", "q", "axial")] * 3 + [("a", "k", "axial")] + [("a", "v", None)]
           + [("b", "q", "1d")] * 3 + [("b", "k", "1d")] * 3 + [("b", "v", None)] * 3
           + [("c", "q", "1d")] * 2 + [("c", "k", "1d")] + [("c", "v", None)])
N_GROUPS = len(_GROUPS)
A_Q_ORDER = (0, 3, 1, 4, 2, 5)
C_Q_ORDER = (0, 2, 1, 3)


def _rms_rows(t):
    return t * lax.rsqrt(jnp.mean(t * t, axis=-1, keepdims=True) + NORM_EPS)


def _proj_kernel(x_ref, g_ref, w_ref, bd_ref, gain_ref, cos_ref, sin_ref,
                 aq_ref, ak_ref, av_ref, b1_ref, b4_ref, b16_ref, cq_ref, ck_ref, cv_ref,
                 scr_ref, *, tm):
    x = x_ref[...]
    h = _rms_rows(x) * g_ref[...]
    proj = jnp.dot(h.astype(BF16), w_ref[...], preferred_element_type=F32)
    bd = bd_ref[...]
    lane = lax.broadcasted_iota(jnp.int32, (1, LANES), 1)

    def norm_rope(y, j, kind):
        sq = y * y
        hi = sq.astype(BF16)
        lo = (sq - hi.astype(F32)).astype(BF16)
        ssq = (jnp.dot(hi, bd, preferred_element_type=F32)
               + jnp.dot(lo, bd, preferred_element_type=F32))
        yn = y * lax.rsqrt(ssq * (1.0 / HEAD_DIM) + NORM_EPS) * gain_ref[j:j + 1, :]
        t, sh = (0, 16) if kind == "axial" else (1, 32)
        first = (lane % (2 * sh)) < sh
        partner = jnp.where(first, pltpu.roll(yn, LANES - sh, 1), pltpu.roll(yn, sh, 1))
        return yn * cos_ref[t] + partner * sin_ref[t]

    outs = {("a", "q"): aq_ref, ("a", "k"): ak_ref, ("a", "v"): av_ref,
            ("c", "q"): cq_ref, ("c", "k"): ck_ref, ("c", "v"): cv_ref}
    pos = {}
    b_col = 0
    for j, (mixer, role, kind) in enumerate(_GROUPS):
        y = proj[:, j * LANES:(j + 1) * LANES]
        if kind is not None:
            y = norm_rope(y, j, kind)
        if mixer == "b":
            scr_ref[b_col] = y
            b1_ref[:, b_col * LANES:(b_col + 1) * LANES] = y.astype(BF16)
            b_col += 1
        else:
            c = pos.get((mixer, role), 0)
            outs[(mixer, role)][:, c * LANES:(c + 1) * LANES] = y.astype(BF16)
            pos[(mixer, role)] = c + 1
    for r, ref in ((4, b4_ref), (16, b16_ref)):
        for c in range(r):
            for g in range(b_col):
                ref[c, :, g * LANES:(g + 1) * LANES] = scr_ref[g, pl.ds(c, tm // r, stride=r), :].astype(BF16)


def _proj_call(x2d, g, w, bd, gains, cos_t, sin_t, *, batch, seq):
    n = x2d.shape[0]
    tm = TOKEN_TILE
    nt = seq // tm
    bw = 9 * LANES
    row = lambda i: (i, 0)
    const2 = lambda i: (0, 0)
    bmap = lambda i: (i // nt, 0, i % nt, 0)
    outs = [
        (jax.ShapeDtypeStruct((n, 3 * LANES), BF16), pl.BlockSpec((tm, 3 * LANES), row)),
        (jax.ShapeDtypeStruct((n, LANES), BF16), pl.BlockSpec((tm, LANES), row)),
        (jax.ShapeDtypeStruct((n, LANES), BF16), pl.BlockSpec((tm, LANES), row)),
        (jax.ShapeDtypeStruct((batch, 1, seq, bw), BF16), pl.BlockSpec((None, None, tm, bw), bmap)),
        (jax.ShapeDtypeStruct((batch, 4, seq // 4, bw), BF16), pl.BlockSpec((None, 4, tm // 4, bw), bmap)),
        (jax.ShapeDtypeStruct((batch, 16, seq // 16, bw), BF16), pl.BlockSpec((None, 16, tm // 16, bw), bmap)),
        (jax.ShapeDtypeStruct((n, 2 * LANES), BF16), pl.BlockSpec((tm, 2 * LANES), row)),
        (jax.ShapeDtypeStruct((n, LANES), BF16), pl.BlockSpec((tm, LANES), row)),
        (jax.ShapeDtypeStruct((n, LANES), BF16), pl.BlockSpec((tm, LANES), row)),
    ]
    return pl.pallas_call(
        functools.partial(_proj_kernel, tm=tm),
        name="proj",
        grid=(n // tm,),
        in_specs=[
            pl.BlockSpec((tm, x2d.shape[1]), row),
            pl.BlockSpec(g.shape, const2),
            pl.BlockSpec(w.shape, const2),
            pl.BlockSpec(bd.shape, const2),
            pl.BlockSpec(gains.shape, const2),
            pl.BlockSpec((2, tm, LANES), lambda i: (0, i % nt, 0)),
            pl.BlockSpec((2, tm, LANES), lambda i: (0, i % nt, 0)),
        ],
        out_specs=[o[1] for o in outs],
        out_shape=[o[0] for o in outs],
        scratch_shapes=[pltpu.VMEM((bw // LANES, tm, LANES), F32)],
        compiler_params=pltpu.CompilerParams(dimension_semantics=("arbitrary",), vmem_limit_bytes=VMEM_LIMIT),
    )(x2d, g, w, bd, gains, cos_t, sin_t)


def _attn_kernel(*refs, n_seq, seq_len, tq, win, half, n_qp, n_kp, has_sink, emit_lse):
    it = iter(refs)
    q_all, k_all, v_all = next(it), next(it), next(it)
    sink_ref = next(it) if has_sink else None
    o_all = next(it)
    lse_all = next(it) if emit_lse else None
    lane = lax.broadcasted_iota(jnp.int32, (1, LANES), 1)
    low = lane < HEAD_DIM
    n_tiles = seq_len // tq

    def tile(i, carry):
        if n_seq:
            sq = i // n_tiles
            q_ref, k_ref, v_ref, o_ref = q_all.at[sq], k_all.at[sq], v_all.at[sq], o_all.at[sq]
            lse_ref = lse_all.at[sq] if emit_lse else None
            r0 = pl.multiple_of((i % n_tiles) * tq, tq)
        else:
            q_ref, k_ref, v_ref, o_ref, lse_ref = q_all, k_all, v_all, o_all, lse_all
            r0 = pl.multiple_of(i * tq, tq)
        if half is None or win == seq_len:
            ks = 0
        else:
            ks = pl.multiple_of(jnp.clip(r0 - half, 0, seq_len - win), half)
        if half is not None:
            d = (r0 - ks) + lax.broadcasted_iota(jnp.int32, (tq, win), 0) - lax.broadcasted_iota(jnp.int32, (tq, win), 1)
            visible = jnp.abs(d) <= half
        for j in range(n_qp):
            kj = j if n_kp > 1 else 0
            q = q_ref[pl.ds(r0, tq), j * LANES:(j + 1) * LANES]
            k = k_ref[pl.ds(ks, win), kj * LANES:(kj + 1) * LANES]
            v = v_ref[pl.ds(ks, win), kj * LANES:(kj + 1) * LANES]
            o_h, lse_h = [], []
            for hf in range(2):
                qm = jnp.where(low if hf == 0 else jnp.logical_not(low), q, jnp.zeros_like(q))
                s = lax.dot_general(qm, k, (((1,), (1,)), ((), ())), preferred_element_type=F32)
                if half is not None:
                    s = jnp.where(visible, s, NEG_INF)
                m = jnp.max(s, axis=-1, keepdims=True)
                if has_sink:
                    sink = sink_ref[2 * j + hf]
                    m = jnp.maximum(m, sink)
                p = jnp.exp(s - m)
                l = jnp.sum(p, axis=-1, keepdims=True)
                if has_sink:
                    l = l + jnp.exp(sink - m)
                o = jnp.dot(p.astype(BF16), v, preferred_element_type=F32)
                o_h.append(o / l)
                if emit_lse:
                    lse_h.append(m + jnp.log(l))
            o_ref[pl.ds(r0, tq), j * LANES:(j + 1) * LANES] = jnp.where(low, o_h[0], o_h[1])
            if emit_lse:
                lse_ref[pl.ds(r0, tq), j * LANES:(j + 1) * LANES] = jnp.where(low, lse_h[0], lse_h[1])
        return carry

    lax.fori_loop(0, max(n_seq, 1) * n_tiles, tile, 0)


def _attn_call(name, q_arg, k_arg, v_arg, grid, q_spec, k_spec, v_spec, out_shape, out_spec, *,
               seq_len, tq, win, half, n_qp, n_kp, n_seq=0, sink=None, emit_lse=False):
    in_specs = [q_spec, k_spec, v_spec]
    args = [q_arg, k_arg, v_arg]
    if sink is not None:
        in_specs.append(pl.BlockSpec(memory_space=pltpu.SMEM))
        args.append(sink)
    n_out = 2 if emit_lse else 1
    res = pl.pallas_call(
        functools.partial(_attn_kernel, n_seq=n_seq, seq_len=seq_len, tq=tq, win=win, half=half, n_qp=n_qp, n_kp=n_kp,
                          has_sink=sink is not None, emit_lse=emit_lse),
        name=name,
        grid=grid,
        in_specs=in_specs,
        out_specs=[out_spec] * n_out,
        out_shape=[out_shape] * n_out,
        compiler_params=pltpu.CompilerParams(dimension_semantics=("arbitrary",) * len(grid),
                                             vmem_limit_bytes=VMEM_LIMIT),
    )(*args)
    return res


def _top_rows(s, payload=None):
    n_rows = s.shape[0]
    riota = lax.broadcasted_iota(jnp.int32, s.shape, 0).astype(F32)
    vals, picks = [], []
    for _ in range(PEER_TOPK):
        m = jnp.max(s, axis=0, keepdims=True)
        am = jnp.min(jnp.where(s == m, riota, float(n_rows)), axis=0, keepdims=True)
        hit = riota == am
        vals.append(m)
        if payload is None:
            picks.append(am)
        else:
            picks.append(jnp.max(jnp.where(hit, payload, -1.0), axis=0, keepdims=True))
        s = jnp.where(hit, -jnp.inf, s)
    return jnp.concatenate(vals, axis=0), jnp.concatenate(picks, axis=0)


def _pair_candidates(t0, t1, combine):
    h = PEER_TOPK // 2
    blocks = [combine(t0[0:1], t1)]
    blocks += [combine(t0[a:a + 1], t1[0:h]) for a in range(1, h)]
    blocks.append(combine(t0[h:], t1[0:1]))
    return jnp.concatenate(blocks, axis=0)


def _out_kernel(x_ref, oa_ref, ob1_ref, l1_ref, ob4_ref, l4_ref, ob16_ref, l16_ref, oc_ref,
                gn_ref, wo_ref, fn_ref, wq_ref, keys_ref,
                x1_ref, hn_ref, idx_ref, gate_ref,
                so4, sl4, so16, sl16, *, tm):
    ng = ob1_ref.shape[1] // LANES
    for r, o_src, l_src, o_dst, l_dst in ((4, ob4_ref, l4_ref, so4, sl4), (16, ob16_ref, l16_ref, so16, sl16)):
        for c in range(r):
            for g in range(ng):
                o_dst[g, pl.ds(c, tm // r, stride=r), :] = o_src[c, :, g * LANES:(g + 1) * LANES]
                l_dst[g, pl.ds(c, tm // r, stride=r), :] = l_src[c, :, g * LANES:(g + 1) * LANES]
    cat = lambda ref: jnp.concatenate([ref[g] for g in range(ng)], axis=-1)
    l1, l4, l16 = l1_ref[...], cat(sl4), cat(sl16)
    mx = jnp.maximum(jnp.maximum(l1, l4), l16)
    e1, e4, e16 = jnp.exp(l1 - mx), jnp.exp(l4 - mx), jnp.exp(l16 - mx)
    ob = (e1 * ob1_ref[...] + e4 * cat(so4) + e16 * cat(so16)) / (e1 + e4 + e16)
    mix = jnp.concatenate([_rms_rows(oa_ref[...]), _rms_rows(ob), _rms_rows(oc_ref[...])], axis=-1) * gn_ref[...]
    x1 = x_ref[...] + jnp.dot(mix.astype(BF16), wo_ref[...], preferred_element_type=F32)
    x1_ref[...] = x1
    hn = _rms_rows(x1) * fn_ref[...]
    hn_ref[...] = hn
    qp = jnp.dot(hn.astype(BF16), wq_ref[...], preferred_element_type=F32)

    idx_rows, gate_rows = [], []
    for h in range(PEER_HEADS):
        qh = qp[:, h * LANES:(h + 1) * LANES].astype(BF16)
        sv, si = [], []
        for p in range(2):
            st = lax.dot_general(keys_ref[h, p], qh, (((1,), (1,)), ((), ())), preferred_element_type=F32)
            v, i = _top_rows(st)
            sv.append(v)
            si.append(i)
        cand = _pair_candidates(sv[0], sv[1], lambda a, b: a + b)
        cidx = _pair_candidates(si[0], si[1], lambda a, b: a * float(N_KEYS) + b)
        best, eidx = _top_rows(cand, payload=cidx)
        e = jnp.exp(best - best[0:1])
        gate_rows.append(e / jnp.sum(e, axis=0, keepdims=True))
        idx_rows.append(eidx)
    idx_ref[...] = jnp.concatenate(idx_rows, axis=0).T.astype(jnp.int32)
    gate_ref[...] = jnp.concatenate(gate_rows, axis=0).T


def _out_call(x2d, oa, ob1, l1, ob4, l4, ob16, l16, oc, gn, wo, fn, wq, keys_p, *, batch, seq):
    n, d = x2d.shape
    tm = TOKEN_TILE
    nt = seq // tm
    bw = 3 * LANES
    row = lambda i: (i, 0)
    const2 = lambda i: (0, 0)
    bmap = lambda i: (i // nt, 0, i % nt, 0)
    n_e = PEER_HEADS * PEER_TOPK
    return pl.pallas_call(
        functools.partial(_out_kernel, tm=tm),
        name="out_peerq",
        grid=(n // tm,),
        in_specs=[
            pl.BlockSpec((tm, d), row),
            pl.BlockSpec((tm, bw), row),
            pl.BlockSpec((None, None, tm, bw), bmap), pl.BlockSpec((None, None, tm, bw), bmap),
            pl.BlockSpec((None, 4, tm // 4, bw), bmap), pl.BlockSpec((None, 4, tm // 4, bw), bmap),
            pl.BlockSpec((None, 16, tm // 16, bw), bmap), pl.BlockSpec((None, 16, tm // 16, bw), bmap),
            pl.BlockSpec((tm, 2 * LANES), row),
            pl.BlockSpec(gn.shape, const2),
            pl.BlockSpec(wo.shape, const2),
            pl.BlockSpec(fn.shape, const2),
            pl.BlockSpec(wq.shape, const2),
            pl.BlockSpec(keys_p.shape, lambda i: (0, 0, 0, 0)),
        ],
        out_specs=[pl.BlockSpec((tm, d), row), pl.BlockSpec((tm, d), row),
                   pl.BlockSpec((tm, n_e), row), pl.BlockSpec((tm, n_e), row)],
        out_shape=[jax.ShapeDtypeStruct((n, d), F32), jax.ShapeDtypeStruct((n, d), F32),
                   jax.ShapeDtypeStruct((n, n_e), jnp.int32), jax.ShapeDtypeStruct((n, n_e), F32)],
        scratch_shapes=[pltpu.VMEM((bw // LANES, tm, LANES), F32)] * 4,
        compiler_params=pltpu.CompilerParams(dimension_semantics=("arbitrary",), vmem_limit_bytes=VMEM_LIMIT),
    )(x2d, oa, ob1, l1, ob4, l4, ob16, l16, oc, gn, wo, fn, wq, keys_p)


def _unpack_row(w):
    lo = pltpu.bitcast(w << 16, F32)
    hi = pltpu.bitcast(w & jnp.uint32(0xFFFF0000), F32)
    return lo, hi


def _idx_pipeline(idx_hbm, idx_smem, sem, nh, body):
    i = pl.program_id(0)
    n = pl.num_programs(0)

    def copy(block, slot):
        return pltpu.make_async_copy(idx_hbm.at[pl.ds(block * nh, nh)], idx_smem.at[slot], sem.at[slot])

    @pl.when(i == 0)
    def _():
        copy(0, 0).start()

    copy(2 * i, 0).wait()
    copy(2 * i + 1, 1).start()
    body(0, idx_smem.at[0])
    copy(2 * i + 1, 1).wait()

    @pl.when(i + 1 < n)
    def _():
        copy(2 * i + 2, 0).start()

    body(1, idx_smem.at[1])


N_ROT = 2


def _peer_u_kernel(idx_hbm, hn_ref, gate_ref, tbl_ref, act_ref, idx_smem, sem, *p_refs, nt, ne):
    nh = nt // 2
    diag = lax.broadcasted_iota(jnp.int32, (ne, ne), 0) == lax.broadcasted_iota(jnp.int32, (ne, ne), 1)

    def finish(t, p_ref):
        q = (p_ref[pl.ds(0, ne, stride=4), :] + p_ref[pl.ds(1, ne, stride=4), :]
             + p_ref[pl.ds(2, ne, stride=4), :] + p_ref[pl.ds(3, ne, stride=4), :])
        s_col = jnp.sum(q, axis=1, keepdims=True)
        s_row = jnp.sum(jnp.where(diag, s_col, 0.0), axis=0, keepdims=True)
        gelu = 0.5 * s_row * (1.0 + lax.erf(s_row * (2.0 ** -0.5)))
        act_ref[pl.ds(t, 1), :] = gelu * gate_ref[pl.ds(t, 1), :]

    def half(hf, idx_ref):
        for tt in range(nh):
            t = hf * nh + tt
            p_ref = p_refs[tt % N_ROT]
            x8 = hn_ref[t]
            xlo, xhi = x8[0:4], x8[4:8]
            for k in range(ne):
                lo, hi = _unpack_row(tbl_ref[idx_ref[tt, k]])
                p_ref[pl.ds(4 * k, 4), :] = lo * xlo + hi * xhi
            if tt >= 1:
                finish(t - 1, p_refs[(tt - 1) % N_ROT])
        finish(hf * nh + nh - 1, p_refs[(nh - 1) % N_ROT])

    _idx_pipeline(idx_hbm, idx_smem, sem, nh, half)


def _peer_v_kernel(idx_hbm, act_ref, x_ref, tbl_ref, o_ref, idx_smem, sem, *r_refs, nt, ne):
    nh = nt // 2

    def replicate(t, r_ref):
        r_ref[...] = jnp.broadcast_to(act_ref[pl.ds(t, 1), :], (ne, ne)).T

    def half(hf, idx_ref):
        replicate(hf * nh, r_refs[0])
        for tt in range(nh):
            t = hf * nh + tt
            if tt + 1 < nh:
                replicate(t + 1, r_refs[(tt + 1) % N_ROT])
            r_ref = r_refs[tt % N_ROT]
            acc = [jnp.zeros((4, LANES), F32) for _ in range(4)]
            for k in range(ne):
                lo, hi = _unpack_row(tbl_ref[idx_ref[tt, k]])
                a = r_ref[pl.ds(k, 1), :]
                j = 2 * (k % 2)
                acc[j], acc[j + 1] = acc[j] + a * lo, acc[j + 1] + a * hi
            o_ref[t] = x_ref[t] + jnp.concatenate([acc[0] + acc[2], acc[1] + acc[3]], axis=0)

    _idx_pipeline(idx_hbm, idx_smem, sem, nh, half)


def _peer_call(x1, hn, idx, gate, tbl_u, tbl_v):
    n, d = x1.shape
    ne = idx.shape[1]
    nt = PEER_TOKENS
    rows = d // LANES
    row2 = lambda i: (i, 0)
    row3 = lambda i: (i, 0, 0)
    tbl_spec = pl.BlockSpec(tbl_u.shape, lambda i: (0, 0, 0), pipeline_mode=pl.Buffered(1))
    idx_spec = pl.BlockSpec(memory_space=pl.ANY)
    idx_scratch = [pltpu.SMEM((2, nt // 2, ne), jnp.int32), pltpu.SemaphoreType.DMA((2,))]
    params = pltpu.CompilerParams(dimension_semantics=("arbitrary",), vmem_limit_bytes=PEER_VMEM_LIMIT)
    act = pl.pallas_call(
        functools.partial(_peer_u_kernel, nt=nt, ne=ne),
        name="peer_u",
        grid=(n // nt,),
        in_specs=[idx_spec, pl.BlockSpec((nt, rows, LANES), row3), pl.BlockSpec((nt, ne), row2), tbl_spec],
        out_specs=pl.BlockSpec((nt, ne), row2),
        out_shape=jax.ShapeDtypeStruct((n, ne), F32),
        scratch_shapes=idx_scratch + [pltpu.VMEM((4 * ne, LANES), F32)] * N_ROT,
        compiler_params=params,
    )(idx, hn.reshape(n, rows, LANES), gate, tbl_u)
    out = pl.pallas_call(
        functools.partial(_peer_v_kernel, nt=nt, ne=ne),
        name="peer_v",
        grid=(n // nt,),
        in_specs=[idx_spec, pl.BlockSpec((nt, ne), row2), pl.BlockSpec((nt, rows, LANES), row3), tbl_spec],
        out_specs=pl.BlockSpec((nt, rows, LANES), row3),
        out_shape=jax.ShapeDtypeStruct((n, rows, LANES), F32),
        scratch_shapes=idx_scratch + [pltpu.VMEM((ne, ne), F32)] * N_ROT,
        compiler_params=params,
    )(idx, act, x1.reshape(n, rows, LANES), tbl_v)
    return out.reshape(n, d)


def _in_proj_perm():
    widths = (A_HEADS, A_KV_HEADS, A_KV_HEADS, B_HEADS, B_HEADS, B_HEADS, C_HEADS, C_KV_HEADS, C_KV_HEADS)
    orders = (A_Q_ORDER, None, None, None, None, None, C_Q_ORDER, None, None)
    cols, base = [], 0
    for w, order in zip(widths, orders):
        for hd in (order if order is not None else range(w)):
            cols.extend(range(base + hd * HEAD_DIM, base + (hd + 1) * HEAD_DIM))
        base += w * HEAD_DIM
    return np.asarray(cols, dtype=np.int32)


def _mix_perm():
    cols = []
    for hd in A_Q_ORDER:
        cols.extend(range(hd * HEAD_DIM, (hd + 1) * HEAD_DIM))
    base = A_HEADS * HEAD_DIM
    cols.extend(range(base, base + B_HEADS * HEAD_DIM))
    base += B_HEADS * HEAD_DIM
    for hd in C_Q_ORDER:
        cols.extend(range(base + hd * HEAD_DIM, base + (hd + 1) * HEAD_DIM))
    return np.asarray(cols, dtype=np.int32)


def _rope_tables(seq):
    pos = jnp.arange(seq, dtype=F32)
    f64 = ROPE_THETA ** (-jnp.arange(0, HEAD_DIM, 2, dtype=F32) / HEAD_DIM)
    ang = pos[:, None] * f64[None, :]
    cos1 = jnp.concatenate([jnp.cos(ang), jnp.cos(ang)], axis=-1)
    sin1 = jnp.concatenate([-jnp.sin(ang), jnp.sin(ang)], axis=-1)
    half = HEAD_DIM // 2
    f32_ = ROPE_THETA ** (-jnp.arange(0, half, 2, dtype=F32) / half)
    rows = seq // GRID_W
    row = jnp.repeat(jnp.arange(rows, dtype=F32), GRID_W)
    col = jnp.tile(jnp.arange(GRID_W, dtype=F32), rows)
    ar, ac = row[:, None] * f32_[None, :], col[:, None] * f32_[None, :]
    cos_a = jnp.concatenate([jnp.cos(ar), jnp.cos(ar), jnp.cos(ac), jnp.cos(ac)], axis=-1)
    sin_a = jnp.concatenate([-jnp.sin(ar), jnp.sin(ar), -jnp.sin(ac), jnp.sin(ac)], axis=-1)
    two = lambda t: jnp.concatenate([t, t], axis=-1)
    return jnp.stack([two(cos_a), two(cos1)]), jnp.stack([two(sin_a), two(sin1)])


def _group_gains(g):
    rows = []
    mix_id = {"a": 0, "b": 1, "c": 2}
    for mixer, role, kind in _GROUPS:
        if kind is None:
            rows.append(jnp.ones((LANES,), F32))
        else:
            gg = g[mix_id[mixer], 0 if role == "q" else 1].astype(F32)
            if role == "q":
                gg = gg * ATTN_SCALE
            rows.append(jnp.concatenate([gg, gg]))
    return jnp.stack(rows)


def _pack_table(t):
    e, d = t.shape
    b = lax.bitcast_convert_type(t.astype(BF16), jnp.uint16).astype(jnp.uint32)
    w = (b[:, d // 2:] << 16) | b[:, :d // 2]
    return w.reshape(e, d // 2 // LANES, LANES)


def _pad_keys(keys):
    z = jnp.zeros_like(keys[:, 0])
    k0 = jnp.concatenate([keys[:, 0], z], axis=-1)
    k1 = jnp.concatenate([z, keys[:, 1]], axis=-1)
    return jnp.stack([k0, k1], axis=1).astype(BF16)


def _layer(x2d, batch, seq, g_attn, w_in, gains, cos_t, sin_t, bd, sink, g_out, w_out, g_ffn, wq, keys_p,
           tbl_u, tbl_v):
    n = x2d.shape[0]
    aq, ak, av, b1, b4, b16, cq, ck, cv = _proj_call(x2d, g_attn, w_in, bd, gains, cos_t, sin_t,
                                                     batch=batch, seq=seq)
    oa = _attn_call(
        "attn_a", aq, ak, av, (batch,),
        pl.BlockSpec((seq, 3 * LANES), lambda b: (b, 0)),
        pl.BlockSpec((seq, LANES), lambda b: (b, 0)),
        pl.BlockSpec((seq, LANES), lambda b: (b, 0)),
        jax.ShapeDtypeStruct((n, 3 * LANES), F32), pl.BlockSpec((seq, 3 * LANES), lambda b: (b, 0)),
        seq_len=seq, tq=256, win=seq, half=None, n_qp=3, n_kp=1)[0]
    ob, lse = [], []
    for (window, r), arr in zip(DILATIONS, (b1, b4, b16)):
        length = seq // r
        half = window // (2 * r)
        tq = min(128, length)
        win = min(tq + 2 * half, length)
        spec = lambda c: pl.BlockSpec((None, r, length, 3 * LANES), lambda b, c=c: (b, 0, 0, c))
        o_r, l_r = _attn_call(
            f"attn_b{r}", arr, arr, arr, (batch,), spec(0), spec(1), spec(2),
            jax.ShapeDtypeStruct((batch, r, length, 3 * LANES), F32), spec(0),
            seq_len=length, tq=tq, win=win, half=half, n_qp=3, n_kp=3, n_seq=r, emit_lse=True)
        ob.append(o_r)
        lse.append(l_r)
    tq = LOCAL_HALF_WINDOW
    oc = _attn_call(
        "attn_c", cq, ck, cv, (batch,),
        pl.BlockSpec((seq, 2 * LANES), lambda b: (b, 0)),
        pl.BlockSpec((seq, LANES), lambda b: (b, 0)),
        pl.BlockSpec((seq, LANES), lambda b: (b, 0)),
        jax.ShapeDtypeStruct((n, 2 * LANES), F32), pl.BlockSpec((seq, 2 * LANES), lambda b: (b, 0)),
        seq_len=seq, tq=tq, win=min(tq + 2 * LOCAL_HALF_WINDOW, seq), half=LOCAL_HALF_WINDOW,
        n_qp=2, n_kp=1, sink=sink)[0]
    x1, hn, idx, gate = _out_call(x2d, oa, ob[0], lse[0], ob[1], lse[1], ob[2], lse[2], oc,
                                  g_out, w_out, g_ffn, wq, keys_p, batch=batch, seq=seq)
    return _peer_call(x1, hn, idx, gate, tbl_u, tbl_v)


def kernel(x, attn_norm, w_in, qk_gain, sink_logits, out_norm, w_out, ffn_norm, peer_wq, peer_keys, peer_u, peer_v):
    batch, seq, d = x.shape
    depth = w_in.shape[0]
    in_perm = _in_proj_perm()
    mix_perm = _mix_perm()
    cos_t, sin_t = _rope_tables(seq)
    bd = jnp.asarray(np.kron(np.eye(2), np.ones((HEAD_DIM, HEAD_DIM))), dtype=BF16)
    sink_perm = np.asarray([C_Q_ORDER[2 * j + hf] for j in range(2) for hf in range(2)], dtype=np.int32)
    x2d = x.reshape(batch * seq, d)
    for layer in range(depth):
        x2d = _layer(
            x2d, batch, seq,
            attn_norm[layer][None, :].astype(F32),
            w_in[layer][:, in_perm].astype(BF16),
            _group_gains(qk_gain[layer]),
            cos_t, sin_t, bd,
            sink_logits[layer][sink_perm].astype(F32),
            out_norm[layer][mix_perm][None, :].astype(F32),
            w_out[layer][mix_perm, :].astype(BF16),
            ffn_norm[layer][None, :].astype(F32),
            peer_wq[layer].astype(BF16),
            _pad_keys(peer_keys[layer]),
            _pack_table(peer_u[layer]),
            _pack_table(peer_v[layer]),
        )
    return x2d.reshape(batch, seq, d)
```

```python
import functools

import numpy as np
import jax
import jax.numpy as jnp
from jax import lax
from jax.experimental import pallas as pl
from jax.experimental.pallas import tpu as pltpu

F32 = jnp.float32
BF16 = jnp.bfloat16

LANES = 128
HEAD_DIM = 64
A_HEADS, A_KV_HEADS = 6, 2
B_HEADS = 6
C_HEADS, C_KV_HEADS = 4, 2
ROPE_THETA = 10000.0
GRID_W = 64
DILATIONS = ((128, 1), (512, 4), (2048, 16))
LOCAL_HALF_WINDOW = 128
PEER_HEADS = 8
N_KEYS = 128
PEER_TOPK = 16
NORM_EPS = 1e-6
NEG_INF = -1e30
ATTN_SCALE = HEAD_DIM ** -0.5

TOKEN_TILE = 256
PEER_TOKENS = 32
EXPERT_ROWS = 4
ATTN_INTERLEAVE = 2
VMEM_LIMIT = 48 * 1024 * 1024
PEER_VMEM_LIMIT = 56 * 1024 * 1024

_GROUPS = ([("a", "q", "axial")] * 3 + [("a", "k", "axial")] + [("a", "v", None)]
           + [("b", "q", "1d")] * 3 + [("b", "k", "1d")] * 3 + [("b", "v", None)] * 3
           + [("c", "q", "1d")] * 2 + [("c", "k", "1d")] + [("c", "v", None)])
N_GROUPS = len(_GROUPS)
A_Q_ORDER = (0, 3, 1, 4, 2, 5)
C_Q_ORDER = (0, 2, 1, 3)


def _rms_rows(t):
    return t * lax.rsqrt(jnp.mean(t * t, axis=-1, keepdims=True) + NORM_EPS)


def _proj_kernel(x_ref, g_ref, w_ref, bd_ref, gain_ref, cos_ref, sin_ref,
                 aq_ref, ak_ref, av_ref, b1_ref, b4_ref, b16_ref, cq_ref, ck_ref, cv_ref,
                 scr_ref, *, tm):
    x = x_ref[...]
    h = _rms_rows(x) * g_ref[...]
    proj = jnp.dot(h.astype(BF16), w_ref[...], preferred_element_type=F32)
    bd = bd_ref[...]
    lane = lax.broadcasted_iota(jnp.int32, (1, LANES), 1)

    def norm_rope(y, j, kind):
        sq = y * y
        hi = sq.astype(BF16)
        lo = (sq - hi.astype(F32)).astype(BF16)
        ssq = (jnp.dot(hi, bd, preferred_element_type=F32)
               + jnp.dot(lo, bd, preferred_element_type=F32))
        yn = y * lax.rsqrt(ssq * (1.0 / HEAD_DIM) + NORM_EPS) * gain_ref[j:j + 1, :]
        t, sh = (0, 16) if kind == "axial" else (1, 32)
        first = (lane % (2 * sh)) < sh
        partner = jnp.where(first, pltpu.roll(yn, LANES - sh, 1), pltpu.roll(yn, sh, 1))
        return yn * cos_ref[t] + partner * sin_ref[t]

    outs = {("a", "q"): aq_ref, ("a", "k"): ak_ref, ("a", "v"): av_ref,
            ("c", "q"): cq_ref, ("c", "k"): ck_ref, ("c", "v"): cv_ref}
    pos = {}
    b_col = 0
    for j, (mixer, role, kind) in enumerate(_GROUPS):
        y = proj[:, j * LANES:(j + 1) * LANES]
        if kind is not None:
            y = norm_rope(y, j, kind)
        if mixer == "b":
            scr_ref[b_col] = y
            b1_ref[:, b_col * LANES:(b_col + 1) * LANES] = y.astype(BF16)
            b_col += 1
        else:
            c = pos.get((mixer, role), 0)
            outs[(mixer, role)][:, c * LANES:(c + 1) * LANES] = y.astype(BF16)
            pos[(mixer, role)] = c + 1
    for r, ref in ((4, b4_ref), (16, b16_ref)):
        for c in range(r):
            for g in range(b_col):
                ref[c, :, g * LANES:(g + 1) * LANES] = scr_ref[g, pl.ds(c, tm // r, stride=r), :].astype(BF16)


def _proj_call(x2d, g, w, bd, gains, cos_t, sin_t, *, batch, seq):
    n = x2d.shape[0]
    tm = TOKEN_TILE
    nt = seq // tm
    bw = 9 * LANES
    row = lambda i: (i, 0)
    const2 = lambda i: (0, 0)
    bmap = lambda i: (i // nt, 0, i % nt, 0)
    outs = [
        (jax.ShapeDtypeStruct((n, 3 * LANES), BF16), pl.BlockSpec((tm, 3 * LANES), row)),
        (jax.ShapeDtypeStruct((n, LANES), BF16), pl.BlockSpec((tm, LANES), row)),
        (jax.ShapeDtypeStruct((n, LANES), BF16), pl.BlockSpec((tm, LANES), row)),
        (jax.ShapeDtypeStruct((batch, 1, seq, bw), BF16), pl.BlockSpec((None, None, tm, bw), bmap)),
        (jax.ShapeDtypeStruct((batch, 4, seq // 4, bw), BF16), pl.BlockSpec((None, 4, tm // 4, bw), bmap)),
        (jax.ShapeDtypeStruct((batch, 16, seq // 16, bw), BF16), pl.BlockSpec((None, 16, tm // 16, bw), bmap)),
        (jax.ShapeDtypeStruct((n, 2 * LANES), BF16), pl.BlockSpec((tm, 2 * LANES), row)),
        (jax.ShapeDtypeStruct((n, LANES), BF16), pl.BlockSpec((tm, LANES), row)),
        (jax.ShapeDtypeStruct((n, LANES), BF16), pl.BlockSpec((tm, LANES), row)),
    ]
    return pl.pallas_call(
        functools.partial(_proj_kernel, tm=tm),
        name="proj",
        grid=(n // tm,),
        in_specs=[
            pl.BlockSpec((tm, x2d.shape[1]), row),
            pl.BlockSpec(g.shape, const2),
            pl.BlockSpec(w.shape, const2),
            pl.BlockSpec(bd.shape, const2),
            pl.BlockSpec(gains.shape, const2),
            pl.BlockSpec((2, tm, LANES), lambda i: (0, i % nt, 0)),
            pl.BlockSpec((2, tm, LANES), lambda i: (0, i % nt, 0)),
        ],
        out_specs=[o[1] for o in outs],
        out_shape=[o[0] for o in outs],
        scratch_shapes=[pltpu.VMEM((bw // LANES, tm, LANES), F32)],
        compiler_params=pltpu.CompilerParams(dimension_semantics=("arbitrary",), vmem_limit_bytes=VMEM_LIMIT),
    )(x2d, g, w, bd, gains, cos_t, sin_t)


def _attn_kernel(*refs, n_seq, seq_len, tq, win, half, n_qp, n_kp, has_sink, emit_lse):
    it = iter(refs)
    q_all, k_all, v_all = next(it), next(it), next(it)
    sink_ref = next(it) if has_sink else None
    o_all = next(it)
    lse_all = next(it) if emit_lse else None
    lane = lax.broadcasted_iota(jnp.int32, (1, LANES), 1)
    low = lane < HEAD_DIM
    n_tiles = seq_len // tq

    total = max(n_seq, 1) * n_tiles
    staged = half is not None
    group = ATTN_INTERLEAVE if (staged and total % ATTN_INTERLEAVE == 0) else 1

    def tile_group(g, carry):
        for u in range(group):
            one_tile(g * group + u)
        return carry

    def one_tile(i):
        if n_seq:
            sq = i // n_tiles
            q_ref, k_ref, v_ref, o_ref = q_all.at[sq], k_all.at[sq], v_all.at[sq], o_all.at[sq]
            lse_ref = lse_all.at[sq] if emit_lse else None
            r0 = pl.multiple_of((i % n_tiles) * tq, tq)
        else:
            q_ref, k_ref, v_ref, o_ref, lse_ref = q_all, k_all, v_all, o_all, lse_all
            r0 = pl.multiple_of(i * tq, tq)
        if half is None or win == seq_len:
            ks = 0
        else:
            ks = pl.multiple_of(jnp.clip(r0 - half, 0, seq_len - win), half)
        if half is not None:
            d = (r0 - ks) + lax.broadcasted_iota(jnp.int32, (tq, win), 0) - lax.broadcasted_iota(jnp.int32, (tq, win), 1)
            visible = jnp.abs(d) <= half
        s_h, v_j = [], []
        for j in range(n_qp):
            kj = j if n_kp > 1 else 0
            q = q_ref[pl.ds(r0, tq), j * LANES:(j + 1) * LANES]
            k = k_ref[pl.ds(ks, win), kj * LANES:(kj + 1) * LANES]
            v_j.append(v_ref[pl.ds(ks, win), kj * LANES:(kj + 1) * LANES])
            for hf in range(2):
                qm = jnp.where(low if hf == 0 else jnp.logical_not(low), q, jnp.zeros_like(q))
                s = lax.dot_general(qm, k, (((1,), (1,)), ((), ())), preferred_element_type=F32)
                s_h.append(jnp.where(visible, s, NEG_INF) if half is not None else s)
        n_h = len(s_h)
        stages = [list(range(n_h))] if staged else [[h] for h in range(n_h)]
        o_h, lse_h = [None] * n_h, [None] * n_h
        for hs in stages:
            m_all = jnp.max(jnp.concatenate([s_h[h] for h in hs], axis=0), axis=-1, keepdims=True)
            m_h, p_h = {}, {}
            for n, h in enumerate(hs):
                m = m_all[n * tq:(n + 1) * tq]
                m_h[h] = jnp.maximum(m, sink_ref[h]) if has_sink else m
                p_h[h] = jnp.exp(s_h[h] - m_h[h])
            l_all = jnp.sum(jnp.concatenate([p_h[h] for h in hs], axis=0), axis=-1, keepdims=True)
            for n, h in enumerate(hs):
                l = l_all[n * tq:(n + 1) * tq]
                if has_sink:
                    l = l + jnp.exp(sink_ref[h] - m_h[h])
                o = jnp.dot(p_h[h].astype(BF16), v_j[h // 2], preferred_element_type=F32)
                o_h[h] = o / l
                if emit_lse:
                    lse_h[h] = m_h[h] + jnp.log(l)
        for j in range(n_qp):
            o_ref[pl.ds(r0, tq), j * LANES:(j + 1) * LANES] = jnp.where(low, o_h[2 * j], o_h[2 * j + 1])
            if emit_lse:
                lse_ref[pl.ds(r0, tq), j * LANES:(j + 1) * LANES] = jnp.where(low, lse_h[2 * j], lse_h[2 * j + 1])

    lax.fori_loop(0, total // group, tile_group, 0)


def _attn_call(name, q_arg, k_arg, v_arg, grid, q_spec, k_spec, v_spec, out_shape, out_spec, *,
               seq_len, tq, win, half, n_qp, n_kp, n_seq=0, sink=None, emit_lse=False):
    in_specs = [q_spec, k_spec, v_spec]
    args = [q_arg, k_arg, v_arg]
    if sink is not None:
        in_specs.append(pl.BlockSpec(memory_space=pltpu.SMEM))
        args.append(sink)
    n_out = 2 if emit_lse else 1
    res = pl.pallas_call(
        functools.partial(_attn_kernel, n_seq=n_seq, seq_len=seq_len, tq=tq, win=win, half=half, n_qp=n_qp, n_kp=n_kp,
                          has_sink=sink is not None, emit_lse=emit_lse),
        name=name,
        grid=grid,
        in_specs=in_specs,
        out_specs=[out_spec] * n_out,
        out_shape=[out_shape] * n_out,
        compiler_params=pltpu.CompilerParams(dimension_semantics=("arbitrary",) * len(grid),
                                             vmem_limit_bytes=VMEM_LIMIT),
    )(*args)
    return res


def _top_rows(s, payload=None):
    n_rows = s.shape[0]
    riota = lax.broadcasted_iota(jnp.int32, s.shape, 0).astype(F32)
    vals, picks = [], []
    for _ in range(PEER_TOPK):
        m = jnp.max(s, axis=0, keepdims=True)
        am = jnp.min(jnp.where(s == m, riota, float(n_rows)), axis=0, keepdims=True)
        hit = riota == am
        vals.append(m)
        if payload is None:
            picks.append(am)
        else:
            picks.append(jnp.max(jnp.where(hit, payload, -1.0), axis=0, keepdims=True))
        s = jnp.where(hit, -jnp.inf, s)
    return jnp.concatenate(vals, axis=0), jnp.concatenate(picks, axis=0)


def _pair_candidates(t0, t1, combine):
    h = PEER_TOPK // 2
    blocks = [combine(t0[0:1], t1)]
    blocks += [combine(t0[a:a + 1], t1[0:h]) for a in range(1, h)]
    blocks.append(combine(t0[h:], t1[0:1]))
    return jnp.concatenate(blocks, axis=0)


def _out_kernel(x_ref, oa_ref, ob1_ref, l1_ref, ob4_ref, l4_ref, ob16_ref, l16_ref, oc_ref,
                gn_ref, wo_ref, fn_ref, wq_ref, keys_ref,
                x1_ref, hn_ref, idx_ref, gate_ref,
                so4, sl4, so16, sl16, *, tm):
    ng = ob1_ref.shape[1] // LANES
    for r, o_src, l_src, o_dst, l_dst in ((4, ob4_ref, l4_ref, so4, sl4), (16, ob16_ref, l16_ref, so16, sl16)):
        for c in range(r):
            for g in range(ng):
                o_dst[g, pl.ds(c, tm // r, stride=r), :] = o_src[c, :, g * LANES:(g + 1) * LANES]
                l_dst[g, pl.ds(c, tm // r, stride=r), :] = l_src[c, :, g * LANES:(g + 1) * LANES]
    cat = lambda ref: jnp.concatenate([ref[g] for g in range(ng)], axis=-1)
    l1, l4, l16 = l1_ref[...], cat(sl4), cat(sl16)
    mx = jnp.maximum(jnp.maximum(l1, l4), l16)
    e1, e4, e16 = jnp.exp(l1 - mx), jnp.exp(l4 - mx), jnp.exp(l16 - mx)
    ob = (e1 * ob1_ref[...] + e4 * cat(so4) + e16 * cat(so16)) / (e1 + e4 + e16)
    mix = jnp.concatenate([_rms_rows(oa_ref[...]), _rms_rows(ob), _rms_rows(oc_ref[...])], axis=-1) * gn_ref[...]
    x1 = x_ref[...] + jnp.dot(mix.astype(BF16), wo_ref[...], preferred_element_type=F32)
    x1_ref[...] = x1
    hn = _rms_rows(x1) * fn_ref[...]
    hn_ref[...] = hn
    qp = jnp.dot(hn.astype(BF16), wq_ref[...], preferred_element_type=F32)

    idx_rows, gate_rows = [], []
    for h in range(PEER_HEADS):
        qh = qp[:, h * LANES:(h + 1) * LANES].astype(BF16)
        sv, si = [], []
        for p in range(2):
            st = lax.dot_general(keys_ref[h, p], qh, (((1,), (1,)), ((), ())), preferred_element_type=F32)
            v, i = _top_rows(st)
            sv.append(v)
            si.append(i)
        cand = _pair_candidates(sv[0], sv[1], lambda a, b: a + b)
        cidx = _pair_candidates(si[0], si[1], lambda a, b: a * float(N_KEYS) + b)
        best, eidx = _top_rows(cand, payload=cidx)
        e = jnp.exp(best - best[0:1])
        gate_rows.append(e / jnp.sum(e, axis=0, keepdims=True))
        idx_rows.append(eidx)
    idx_ref[...] = (jnp.concatenate(idx_rows, axis=0).T * float(EXPERT_ROWS)).astype(jnp.int32)
    gate_ref[...] = jnp.concatenate(gate_rows, axis=0).T


def _out_call(x2d, oa, ob1, l1, ob4, l4, ob16, l16, oc, gn, wo, fn, wq, keys_p, *, batch, seq):
    n, d = x2d.shape
    tm = TOKEN_TILE
    nt = seq // tm
    bw = 3 * LANES
    row = lambda i: (i, 0)
    const2 = lambda i: (0, 0)
    bmap = lambda i: (i // nt, 0, i % nt, 0)
    n_e = PEER_HEADS * PEER_TOPK
    return pl.pallas_call(
        functools.partial(_out_kernel, tm=tm),
        name="out_peerq",
        grid=(n // tm,),
        in_specs=[
            pl.BlockSpec((tm, d), row),
            pl.BlockSpec((tm, bw), row),
            pl.BlockSpec((None, None, tm, bw), bmap), pl.BlockSpec((None, None, tm, bw), bmap),
            pl.BlockSpec((None, 4, tm // 4, bw), bmap), pl.BlockSpec((None, 4, tm // 4, bw), bmap),
            pl.BlockSpec((None, 16, tm // 16, bw), bmap), pl.BlockSpec((None, 16, tm // 16, bw), bmap),
            pl.BlockSpec((tm, 2 * LANES), row),
            pl.BlockSpec(gn.shape, const2),
            pl.BlockSpec(wo.shape, const2),
            pl.BlockSpec(fn.shape, const2),
            pl.BlockSpec(wq.shape, const2),
            pl.BlockSpec(keys_p.shape, lambda i: (0, 0, 0, 0)),
        ],
        out_specs=[pl.BlockSpec((tm, d), row), pl.BlockSpec((tm, d), row),
                   pl.BlockSpec((tm, n_e), row), pl.BlockSpec((tm, n_e), row)],
        out_shape=[jax.ShapeDtypeStruct((n, d), F32), jax.ShapeDtypeStruct((n, d), F32),
                   jax.ShapeDtypeStruct((n, n_e), jnp.int32), jax.ShapeDtypeStruct((n, n_e), F32)],
        scratch_shapes=[pltpu.VMEM((bw // LANES, tm, LANES), F32)] * 4,
        compiler_params=pltpu.CompilerParams(dimension_semantics=("arbitrary",), vmem_limit_bytes=VMEM_LIMIT),
    )(x2d, oa, ob1, l1, ob4, l4, ob16, l16, oc, gn, wo, fn, wq, keys_p)


def _unpack_row(w):
    lo = pltpu.bitcast(w << 16, F32)
    hi = pltpu.bitcast(w & jnp.uint32(0xFFFF0000), F32)
    return lo, hi


def _idx_pipeline(idx_hbm, idx_smem, sem, nh, body):
    i = pl.program_id(0)
    n = pl.num_programs(0)

    def copy(block, slot):
        return pltpu.make_async_copy(idx_hbm.at[pl.ds(block * nh, nh)], idx_smem.at[slot], sem.at[slot])

    @pl.when(i == 0)
    def _():
        copy(0, 0).start()

    copy(2 * i, 0).wait()
    copy(2 * i + 1, 1).start()
    body(0, idx_smem.at[0])
    copy(2 * i + 1, 1).wait()

    @pl.when(i + 1 < n)
    def _():
        copy(2 * i + 2, 0).start()

    body(1, idx_smem.at[1])


N_ROT = 2


def _expert_row(tbl_ref, row):
    return tbl_ref[pl.ds(pl.multiple_of(row, EXPERT_ROWS), EXPERT_ROWS), :]


def _peer_u_kernel(idx_hbm, hn_ref, gate_ref, tbl_ref, act_ref, idx_smem, sem, *p_refs, nt, ne):
    nh = nt // 2
    diag = lax.broadcasted_iota(jnp.int32, (ne, ne), 0) == lax.broadcasted_iota(jnp.int32, (ne, ne), 1)

    def finish(t, p_ref):
        q = (p_ref[pl.ds(0, ne, stride=4), :] + p_ref[pl.ds(1, ne, stride=4), :]
             + p_ref[pl.ds(2, ne, stride=4), :] + p_ref[pl.ds(3, ne, stride=4), :])
        s_col = jnp.sum(q, axis=1, keepdims=True)
        s_row = jnp.sum(jnp.where(diag, s_col, 0.0), axis=0, keepdims=True)
        gelu = 0.5 * s_row * (1.0 + lax.erf(s_row * (2.0 ** -0.5)))
        act_ref[pl.ds(t, 1), :] = gelu * gate_ref[pl.ds(t, 1), :]

    def half(hf, idx_ref):
        for tt in range(nh):
            t = hf * nh + tt
            p_ref = p_refs[tt % N_ROT]
            x8 = hn_ref[t]
            xlo, xhi = x8[0:EXPERT_ROWS], x8[EXPERT_ROWS:]
            for k in range(ne):
                lo, hi = _unpack_row(_expert_row(tbl_ref, idx_ref[tt, k]))
                p_ref[pl.ds(EXPERT_ROWS * k, EXPERT_ROWS), :] = lo * xlo + hi * xhi
            if tt >= 1:
                finish(t - 1, p_refs[(tt - 1) % N_ROT])
        finish(hf * nh + nh - 1, p_refs[(nh - 1) % N_ROT])

    _idx_pipeline(idx_hbm, idx_smem, sem, nh, half)


def _peer_v_kernel(idx_hbm, act_ref, x_ref, tbl_ref, o_ref, idx_smem, sem, *r_refs, nt, ne):
    nh = nt // 2

    def replicate(t, r_ref):
        r_ref[...] = jnp.broadcast_to(act_ref[pl.ds(t, 1), :], (ne, ne)).T

    def half(hf, idx_ref):
        replicate(hf * nh, r_refs[0])
        for tt in range(nh):
            t = hf * nh + tt
            if tt + 1 < nh:
                replicate(t + 1, r_refs[(tt + 1) % N_ROT])
            r_ref = r_refs[tt % N_ROT]
            acc = [jnp.zeros((EXPERT_ROWS, LANES), F32) for _ in range(4)]
            for k in range(ne):
                lo, hi = _unpack_row(_expert_row(tbl_ref, idx_ref[tt, k]))
                a = r_ref[pl.ds(k, 1), :]
                j = 2 * (k % 2)
                acc[j], acc[j + 1] = acc[j] + a * lo, acc[j + 1] + a * hi
            o_ref[t] = x_ref[t] + jnp.concatenate([acc[0] + acc[2], acc[1] + acc[3]], axis=0)

    _idx_pipeline(idx_hbm, idx_smem, sem, nh, half)


def _peer_call(x1, hn, idx, gate, tbl_u, tbl_v):
    n, d = x1.shape
    ne = idx.shape[1]
    nt = PEER_TOKENS
    rows = d // LANES
    row2 = lambda i: (i, 0)
    row3 = lambda i: (i, 0, 0)
    tbl_spec = pl.BlockSpec(tbl_u.shape, lambda i: (0, 0), pipeline_mode=pl.Buffered(1))
    idx_spec = pl.BlockSpec(memory_space=pl.ANY)
    idx_scratch = [pltpu.SMEM((2, nt // 2, ne), jnp.int32), pltpu.SemaphoreType.DMA((2,))]
    params = pltpu.CompilerParams(dimension_semantics=("arbitrary",), vmem_limit_bytes=PEER_VMEM_LIMIT)
    act = pl.pallas_call(
        functools.partial(_peer_u_kernel, nt=nt, ne=ne),
        name="peer_u",
        grid=(n // nt,),
        in_specs=[idx_spec, pl.BlockSpec((nt, rows, LANES), row3), pl.BlockSpec((nt, ne), row2), tbl_spec],
        out_specs=pl.BlockSpec((nt, ne), row2),
        out_shape=jax.ShapeDtypeStruct((n, ne), F32),
        scratch_shapes=idx_scratch + [pltpu.VMEM((4 * ne, LANES), F32)] * N_ROT,
        compiler_params=params,
    )(idx, hn.reshape(n, rows, LANES), gate, tbl_u)
    out = pl.pallas_call(
        functools.partial(_peer_v_kernel, nt=nt, ne=ne),
        name="peer_v",
        grid=(n // nt,),
        in_specs=[idx_spec, pl.BlockSpec((nt, ne), row2), pl.BlockSpec((nt, rows, LANES), row3), tbl_spec],
        out_specs=pl.BlockSpec((nt, rows, LANES), row3),
        out_shape=jax.ShapeDtypeStruct((n, rows, LANES), F32),
        scratch_shapes=idx_scratch + [pltpu.VMEM((ne, ne), F32)] * N_ROT,
        compiler_params=params,
    )(idx, act, x1.reshape(n, rows, LANES), tbl_v)
    return out.reshape(n, d)


def _in_proj_perm():
    widths = (A_HEADS, A_KV_HEADS, A_KV_HEADS, B_HEADS, B_HEADS, B_HEADS, C_HEADS, C_KV_HEADS, C_KV_HEADS)
    orders = (A_Q_ORDER, None, None, None, None, None, C_Q_ORDER, None, None)
    cols, base = [], 0
    for w, order in zip(widths, orders):
        for hd in (order if order is not None else range(w)):
            cols.extend(range(base + hd * HEAD_DIM, base + (hd + 1) * HEAD_DIM))
        base += w * HEAD_DIM
    return np.asarray(cols, dtype=np.int32)


def _mix_perm():
    cols = []
    for hd in A_Q_ORDER:
        cols.extend(range(hd * HEAD_DIM, (hd + 1) * HEAD_DIM))
    base = A_HEADS * HEAD_DIM
    cols.extend(range(base, base + B_HEADS * HEAD_DIM))
    base += B_HEADS * HEAD_DIM
    for hd in C_Q_ORDER:
        cols.extend(range(base + hd * HEAD_DIM, base + (hd + 1) * HEAD_DIM))
    return np.asarray(cols, dtype=np.int32)


def _rope_tables(seq):
    pos = jnp.arange(seq, dtype=F32)
    f64 = ROPE_THETA ** (-jnp.arange(0, HEAD_DIM, 2, dtype=F32) / HEAD_DIM)
    ang = pos[:, None] * f64[None, :]
    cos1 = jnp.concatenate([jnp.cos(ang), jnp.cos(ang)], axis=-1)
    sin1 = jnp.concatenate([-jnp.sin(ang), jnp.sin(ang)], axis=-1)
    half = HEAD_DIM // 2
    f32_ = ROPE_THETA ** (-jnp.arange(0, half, 2, dtype=F32) / half)
    rows = seq // GRID_W
    row = jnp.repeat(jnp.arange(rows, dtype=F32), GRID_W)
    col = jnp.tile(jnp.arange(GRID_W, dtype=F32), rows)
    ar, ac = row[:, None] * f32_[None, :], col[:, None] * f32_[None, :]
    cos_a = jnp.concatenate([jnp.cos(ar), jnp.cos(ar), jnp.cos(ac), jnp.cos(ac)], axis=-1)
    sin_a = jnp.concatenate([-jnp.sin(ar), jnp.sin(ar), -jnp.sin(ac), jnp.sin(ac)], axis=-1)
    two = lambda t: jnp.concatenate([t, t], axis=-1)
    return jnp.stack([two(cos_a), two(cos1)]), jnp.stack([two(sin_a), two(sin1)])


def _group_gains(g):
    rows = []
    mix_id = {"a": 0, "b": 1, "c": 2}
    for mixer, role, kind in _GROUPS:
        if kind is None:
            rows.append(jnp.ones((LANES,), F32))
        else:
            gg = g[mix_id[mixer], 0 if role == "q" else 1].astype(F32)
            if role == "q":
                gg = gg * ATTN_SCALE
            rows.append(jnp.concatenate([gg, gg]))
    return jnp.stack(rows)


def _pack_table(t):
    e, d = t.shape
    assert d == 2 * EXPERT_ROWS * LANES
    b = lax.bitcast_convert_type(t.astype(BF16), jnp.uint16).astype(jnp.uint32)
    w = (b[:, d // 2:] << 16) | b[:, :d // 2]
    return w.reshape(e * EXPERT_ROWS, LANES)


def _pad_keys(keys):
    z = jnp.zeros_like(keys[:, 0])
    k0 = jnp.concatenate([keys[:, 0], z], axis=-1)
    k1 = jnp.concatenate([z, keys[:, 1]], axis=-1)
    return jnp.stack([k0, k1], axis=1).astype(BF16)


def _layer(x2d, batch, seq, g_attn, w_in, gains, cos_t, sin_t, bd, sink, g_out, w_out, g_ffn, wq, keys_p,
           tbl_u, tbl_v):
    n = x2d.shape[0]
    aq, ak, av, b1, b4, b16, cq, ck, cv = _proj_call(x2d, g_attn, w_in, bd, gains, cos_t, sin_t,
                                                     batch=batch, seq=seq)
    oa = _attn_call(
        "attn_a", aq, ak, av, (batch,),
        pl.BlockSpec((seq, 3 * LANES), lambda b: (b, 0)),
        pl.BlockSpec((seq, LANES), lambda b: (b, 0)),
        pl.BlockSpec((seq, LANES), lambda b: (b, 0)),
        jax.ShapeDtypeStruct((n, 3 * LANES), F32), pl.BlockSpec((seq, 3 * LANES), lambda b: (b, 0)),
        seq_len=seq, tq=256, win=seq, half=None, n_qp=3, n_kp=1)[0]
    ob, lse = [], []
    for (window, r), arr in zip(DILATIONS, (b1, b4, b16)):
        length = seq // r
        half = window // (2 * r)
        tq = min(128, length)
        win = min(tq + 2 * half, length)
        spec = lambda c: pl.BlockSpec((None, r, length, 3 * LANES), lambda b, c=c: (b, 0, 0, c))
        o_r, l_r = _attn_call(
            f"attn_b{r}", arr, arr, arr, (batch,), spec(0), spec(1), spec(2),
            jax.ShapeDtypeStruct((batch, r, length, 3 * LANES), F32), spec(0),
            seq_len=length, tq=tq, win=win, half=half, n_qp=3, n_kp=3, n_seq=r, emit_lse=True)
        ob.append(o_r)
        lse.append(l_r)
    tq = LOCAL_HALF_WINDOW
    oc = _attn_call(
        "attn_c", cq, ck, cv, (batch,),
        pl.BlockSpec((seq, 2 * LANES), lambda b: (b, 0)),
        pl.BlockSpec((seq, LANES), lambda b: (b, 0)),
        pl.BlockSpec((seq, LANES), lambda b: (b, 0)),
        jax.ShapeDtypeStruct((n, 2 * LANES), F32), pl.BlockSpec((seq, 2 * LANES), lambda b: (b, 0)),
        seq_len=seq, tq=tq, win=min(tq + 2 * LOCAL_HALF_WINDOW, seq), half=LOCAL_HALF_WINDOW,
        n_qp=2, n_kp=1, sink=sink)[0]
    x1, hn, idx, gate = _out_call(x2d, oa, ob[0], lse[0], ob[1], lse[1], ob[2], lse[2], oc,
                                  g_out, w_out, g_ffn, wq, keys_p, batch=batch, seq=seq)
    return _peer_call(x1, hn, idx, gate, tbl_u, tbl_v)


def kernel(x, attn_norm, w_in, qk_gain, sink_logits, out_norm, w_out, ffn_norm, peer_wq, peer_keys, peer_u, peer_v):
    batch, seq, d = x.shape
    depth = w_in.shape[0]
    in_perm = _in_proj_perm()
    mix_perm = _mix_perm()
    cos_t, sin_t = _rope_tables(seq)
    bd = jnp.asarray(np.kron(np.eye(2), np.ones((HEAD_DIM, HEAD_DIM))), dtype=BF16)
    sink_perm = np.asarray([C_Q_ORDER[2 * j + hf] for j in range(2) for hf in range(2)], dtype=np.int32)
    x2d = x.reshape(batch * seq, d)
    for layer in range(depth):
        x2d = _layer(
            x2d, batch, seq,
            attn_norm[layer][None, :].astype(F32),
            w_in[layer][:, in_perm].astype(BF16),
            _group_gains(qk_gain[layer]),
            cos_t, sin_t, bd,
            sink_logits[layer][sink_perm].astype(F32),
            out_norm[layer][mix_perm][None, :].astype(F32),
            w_out[layer][mix_perm, :].astype(BF16),
            ffn_norm[layer][None, :].astype(F32),
            peer_wq[layer].astype(BF16),
            _pad_keys(peer_keys[layer]),
            _pack_table(peer_u[layer]),
            _pack_table(peer_v[layer]),
        )
    return x2d.reshape(batch, seq, d)
```

```python
import functools

import numpy as np
import jax
import jax.numpy as jnp
from jax import lax
from jax.experimental import pallas as pl
from jax.experimental.pallas import tpu as pltpu

F32 = jnp.float32
BF16 = jnp.bfloat16

LANES = 128
HEAD_DIM = 64
A_HEADS, A_KV_HEADS = 6, 2
B_HEADS = 6
C_HEADS, C_KV_HEADS = 4, 2
ROPE_THETA = 10000.0
GRID_W = 64
DILATIONS = ((128, 1), (512, 4), (2048, 16))
LOCAL_HALF_WINDOW = 128
PEER_HEADS = 8
N_KEYS = 128
PEER_TOPK = 16
NORM_EPS = 1e-6
NEG_INF = -1e30
ATTN_SCALE = HEAD_DIM ** -0.5

TOKEN_TILE = 256
PEER_TOKENS = 32
EXPERT_ROWS = 4
ATTN_INTERLEAVE = 2
VMEM_LIMIT = 48 * 1024 * 1024
PEER_VMEM_LIMIT = 56 * 1024 * 1024

_GROUPS = ([("a", "q", "axial")] * 3 + [("a", "k", "axial")] + [("a", "v", None)]
           + [("b", "q", "1d")] * 3 + [("b", "k", "1d")] * 3 + [("b", "v", None)] * 3
           + [("c", "q", "1d")] * 2 + [("c", "k", "1d")] + [("c", "v", None)])
N_GROUPS = len(_GROUPS)
A_Q_ORDER = (0, 3, 1, 4, 2, 5)
C_Q_ORDER = (0, 2, 1, 3)


def _rms_rows(t):
    return t * lax.rsqrt(jnp.mean(t * t, axis=-1, keepdims=True) + NORM_EPS)


def _proj_kernel(x_ref, g_ref, w_ref, bd_ref, gain_ref, cos_ref, sin_ref,
                 aq_ref, ak_ref, av_ref, b1_ref, b4_ref, b16_ref, cq_ref, ck_ref, cv_ref,
                 scr_ref, *, tm):
    x = x_ref[...]
    h = _rms_rows(x) * g_ref[...]
    proj = jnp.dot(h.astype(BF16), w_ref[...], preferred_element_type=F32)
    bd = bd_ref[...]
    lane = lax.broadcasted_iota(jnp.int32, (1, LANES), 1)

    def norm_rope(y, j, kind):
        sq = y * y
        hi = sq.astype(BF16)
        lo = (sq - hi.astype(F32)).astype(BF16)
        ssq = (jnp.dot(hi, bd, preferred_element_type=F32)
               + jnp.dot(lo, bd, preferred_element_type=F32))
        yn = y * lax.rsqrt(ssq * (1.0 / HEAD_DIM) + NORM_EPS) * gain_ref[j:j + 1, :]
        t, sh = (0, 16) if kind == "axial" else (1, 32)
        first = (lane % (2 * sh)) < sh
        partner = jnp.where(first, pltpu.roll(yn, LANES - sh, 1), pltpu.roll(yn, sh, 1))
        return yn * cos_ref[t] + partner * sin_ref[t]

    outs = {("a", "q"): aq_ref, ("a", "k"): ak_ref, ("a", "v"): av_ref,
            ("c", "q"): cq_ref, ("c", "k"): ck_ref, ("c", "v"): cv_ref}
    pos = {}
    b_col = 0
    for j, (mixer, role, kind) in enumerate(_GROUPS):
        y = proj[:, j * LANES:(j + 1) * LANES]
        if kind is not None:
            y = norm_rope(y, j, kind)
        if mixer == "b":
            scr_ref[b_col] = y
            b1_ref[:, b_col * LANES:(b_col + 1) * LANES] = y.astype(BF16)
            b_col += 1
        else:
            c = pos.get((mixer, role), 0)
            outs[(mixer, role)][:, c * LANES:(c + 1) * LANES] = y.astype(BF16)
            pos[(mixer, role)] = c + 1
    for r, ref in ((4, b4_ref), (16, b16_ref)):
        for c in range(r):
            for g in range(b_col):
                ref[c, :, g * LANES:(g + 1) * LANES] = scr_ref[g, pl.ds(c, tm // r, stride=r), :].astype(BF16)


def _proj_call(x2d, g, w, bd, gains, cos_t, sin_t, *, batch, seq):
    n = x2d.shape[0]
    tm = TOKEN_TILE
    nt = seq // tm
    bw = 9 * LANES
    row = lambda i: (i, 0)
    const2 = lambda i: (0, 0)
    bmap = lambda i: (i // nt, 0, i % nt, 0)
    outs = [
        (jax.ShapeDtypeStruct((n, 3 * LANES), BF16), pl.BlockSpec((tm, 3 * LANES), row)),
        (jax.ShapeDtypeStruct((n, LANES), BF16), pl.BlockSpec((tm, LANES), row)),
        (jax.ShapeDtypeStruct((n, LANES), BF16), pl.BlockSpec((tm, LANES), row)),
        (jax.ShapeDtypeStruct((batch, 1, seq, bw), BF16), pl.BlockSpec((None, None, tm, bw), bmap)),
        (jax.ShapeDtypeStruct((batch, 4, seq // 4, bw), BF16), pl.BlockSpec((None, 4, tm // 4, bw), bmap)),
        (jax.ShapeDtypeStruct((batch, 16, seq // 16, bw), BF16), pl.BlockSpec((None, 16, tm // 16, bw), bmap)),
        (jax.ShapeDtypeStruct((n, 2 * LANES), BF16), pl.BlockSpec((tm, 2 * LANES), row)),
        (jax.ShapeDtypeStruct((n, LANES), BF16), pl.BlockSpec((tm, LANES), row)),
        (jax.ShapeDtypeStruct((n, LANES), BF16), pl.BlockSpec((tm, LANES), row)),
    ]
    return pl.pallas_call(
        functools.partial(_proj_kernel, tm=tm),
        name="proj",
        grid=(n // tm,),
        in_specs=[
            pl.BlockSpec((tm, x2d.shape[1]), row),
            pl.BlockSpec(g.shape, const2),
            pl.BlockSpec(w.shape, const2),
            pl.BlockSpec(bd.shape, const2),
            pl.BlockSpec(gains.shape, const2),
            pl.BlockSpec((2, tm, LANES), lambda i: (0, i % nt, 0)),
            pl.BlockSpec((2, tm, LANES), lambda i: (0, i % nt, 0)),
        ],
        out_specs=[o[1] for o in outs],
        out_shape=[o[0] for o in outs],
        scratch_shapes=[pltpu.VMEM((bw // LANES, tm, LANES), F32)],
        compiler_params=pltpu.CompilerParams(dimension_semantics=("arbitrary",), vmem_limit_bytes=VMEM_LIMIT),
    )(x2d, g, w, bd, gains, cos_t, sin_t)


def _attn_kernel(*refs, n_seq, seq_len, tq, win, half, n_qp, n_kp, has_sink, emit_lse):
    it = iter(refs)
    q_all, k_all, v_all = next(it), next(it), next(it)
    sink_ref = next(it) if has_sink else None
    o_all = next(it)
    lse_all = next(it) if emit_lse else None
    lane = lax.broadcasted_iota(jnp.int32, (1, LANES), 1)
    low = lane < HEAD_DIM
    n_tiles = seq_len // tq

    total = max(n_seq, 1) * n_tiles
    staged = half is not None
    group = ATTN_INTERLEAVE if (staged and total % ATTN_INTERLEAVE == 0) else 1

    def tile_group(g, carry):
        for u in range(group):
            one_tile(g * group + u)
        return carry

    def one_tile(i):
        if n_seq:
            sq = i // n_tiles
            q_ref, k_ref, v_ref, o_ref = q_all.at[sq], k_all.at[sq], v_all.at[sq], o_all.at[sq]
            lse_ref = lse_all.at[sq] if emit_lse else None
            r0 = pl.multiple_of((i % n_tiles) * tq, tq)
        else:
            q_ref, k_ref, v_ref, o_ref, lse_ref = q_all, k_all, v_all, o_all, lse_all
            r0 = pl.multiple_of(i * tq, tq)
        if half is None or win == seq_len:
            ks = 0
        else:
            ks = pl.multiple_of(jnp.clip(r0 - half, 0, seq_len - win), half)
        if half is not None:
            d = (r0 - ks) + lax.broadcasted_iota(jnp.int32, (tq, win), 0) - lax.broadcasted_iota(jnp.int32, (tq, win), 1)
            visible = jnp.abs(d) <= half
        s_h, v_j = [], []
        for j in range(n_qp):
            kj = j if n_kp > 1 else 0
            q = q_ref[pl.ds(r0, tq), j * LANES:(j + 1) * LANES]
            k = k_ref[pl.ds(ks, win), kj * LANES:(kj + 1) * LANES]
            v_j.append(v_ref[pl.ds(ks, win), kj * LANES:(kj + 1) * LANES])
            for hf in range(2):
                qm = jnp.where(low if hf == 0 else jnp.logical_not(low), q, jnp.zeros_like(q))
                s = lax.dot_general(qm, k, (((1,), (1,)), ((), ())), preferred_element_type=F32)
                s_h.append(jnp.where(visible, s, NEG_INF) if half is not None else s)
        n_h = len(s_h)
        stages = [list(range(n_h))] if staged else [[h] for h in range(n_h)]
        o_h, lse_h = [None] * n_h, [None] * n_h
        for hs in stages:
            m_all = jnp.max(jnp.concatenate([s_h[h] for h in hs], axis=0), axis=-1, keepdims=True)
            m_h, p_h = {}, {}
            for n, h in enumerate(hs):
                m = m_all[n * tq:(n + 1) * tq]
                m_h[h] = jnp.maximum(m, sink_ref[h]) if has_sink else m
                p_h[h] = jnp.exp(s_h[h] - m_h[h])
            l_all = jnp.sum(jnp.concatenate([p_h[h] for h in hs], axis=0), axis=-1, keepdims=True)
            for n, h in enumerate(hs):
                l = l_all[n * tq:(n + 1) * tq]
                if has_sink:
                    l = l + jnp.exp(sink_ref[h] - m_h[h])
                o = jnp.dot(p_h[h].astype(BF16), v_j[h // 2], preferred_element_type=F32)
                o_h[h] = o / l
                if emit_lse:
                    lse_h[h] = m_h[h] + jnp.log(l)
        for j in range(n_qp):
            o_ref[pl.ds(r0, tq), j * LANES:(j + 1) * LANES] = jnp.where(low, o_h[2 * j], o_h[2 * j + 1])
            if emit_lse:
                lse_ref[pl.ds(r0, tq), j * LANES:(j + 1) * LANES] = jnp.where(low, lse_h[2 * j], lse_h[2 * j + 1])

    lax.fori_loop(0, total // group, tile_group, 0)


def _attn_call(name, q_arg, k_arg, v_arg, grid, q_spec, k_spec, v_spec, out_shape, out_spec, *,
               seq_len, tq, win, half, n_qp, n_kp, n_seq=0, sink=None, emit_lse=False):
    in_specs = [q_spec, k_spec, v_spec]
    args = [q_arg, k_arg, v_arg]
    if sink is not None:
        in_specs.append(pl.BlockSpec(memory_space=pltpu.SMEM))
        args.append(sink)
    n_out = 2 if emit_lse else 1
    res = pl.pallas_call(
        functools.partial(_attn_kernel, n_seq=n_seq, seq_len=seq_len, tq=tq, win=win, half=half, n_qp=n_qp, n_kp=n_kp,
                          has_sink=sink is not None, emit_lse=emit_lse),
        name=name,
        grid=grid,
        in_specs=in_specs,
        out_specs=[out_spec] * n_out,
        out_shape=[out_shape] * n_out,
        compiler_params=pltpu.CompilerParams(dimension_semantics=("arbitrary",) * len(grid),
                                             vmem_limit_bytes=VMEM_LIMIT),
    )(*args)
    return res


def _top_rows(s, payload=None):
    n_rows = s.shape[0]
    riota = lax.broadcasted_iota(jnp.int32, s.shape, 0).astype(F32)
    vals, picks = [], []
    for _ in range(PEER_TOPK):
        m = jnp.max(s, axis=0, keepdims=True)
        am = jnp.min(jnp.where(s == m, riota, float(n_rows)), axis=0, keepdims=True)
        hit = riota == am
        vals.append(m)
        if payload is None:
            picks.append(am)
        else:
            picks.append(jnp.max(jnp.where(hit, payload, -1.0), axis=0, keepdims=True))
        s = jnp.where(hit, -jnp.inf, s)
    return jnp.concatenate(vals, axis=0), jnp.concatenate(picks, axis=0)


def _pair_candidates(t0, t1, combine):
    h = PEER_TOPK // 2
    blocks = [combine(t0[0:1], t1)]
    blocks += [combine(t0[a:a + 1], t1[0:h]) for a in range(1, h)]
    blocks.append(combine(t0[h:], t1[0:1]))
    return jnp.concatenate(blocks, axis=0)


def _out_kernel(x_ref, oa_ref, ob1_ref, l1_ref, ob4_ref, l4_ref, ob16_ref, l16_ref, oc_ref,
                gn_ref, wo_ref, fn_ref, wq_ref, keys_ref,
                x1_ref, hn_ref, idx_ref, gate_ref,
                so4, sl4, so16, sl16, *, tm):
    ng = ob1_ref.shape[1] // LANES
    for r, o_src, l_src, o_dst, l_dst in ((4, ob4_ref, l4_ref, so4, sl4), (16, ob16_ref, l16_ref, so16, sl16)):
        for c in range(r):
            for g in range(ng):
                o_dst[g, pl.ds(c, tm // r, stride=r), :] = o_src[c, :, g * LANES:(g + 1) * LANES]
                l_dst[g, pl.ds(c, tm // r, stride=r), :] = l_src[c, :, g * LANES:(g + 1) * LANES]
    cat = lambda ref: jnp.concatenate([ref[g] for g in range(ng)], axis=-1)
    l1, l4, l16 = l1_ref[...], cat(sl4), cat(sl16)
    mx = jnp.maximum(jnp.maximum(l1, l4), l16)
    e1, e4, e16 = jnp.exp(l1 - mx), jnp.exp(l4 - mx), jnp.exp(l16 - mx)
    ob = (e1 * ob1_ref[...] + e4 * cat(so4) + e16 * cat(so16)) / (e1 + e4 + e16)
    mix = jnp.concatenate([_rms_rows(oa_ref[...]), _rms_rows(ob), _rms_rows(oc_ref[...])], axis=-1) * gn_ref[...]
    x1 = x_ref[...] + jnp.dot(mix.astype(BF16), wo_ref[...], preferred_element_type=F32)
    x1_ref[...] = x1
    hn = _rms_rows(x1) * fn_ref[...]
    hn_ref[...] = hn
    qp = jnp.dot(hn.astype(BF16), wq_ref[...], preferred_element_type=F32)

    idx_rows, gate_rows = [], []
    for h in range(PEER_HEADS):
        qh = qp[:, h * LANES:(h + 1) * LANES].astype(BF16)
        sv, si = [], []
        for p in range(2):
            st = lax.dot_general(keys_ref[h, p], qh, (((1,), (1,)), ((), ())), preferred_element_type=F32)
            v, i = _top_rows(st)
            sv.append(v)
            si.append(i)
        cand = _pair_candidates(sv[0], sv[1], lambda a, b: a + b)
        cidx = _pair_candidates(si[0], si[1], lambda a, b: a * float(N_KEYS) + b)
        best, eidx = _top_rows(cand, payload=cidx)
        e = jnp.exp(best - best[0:1])
        gate_rows.append(e / jnp.sum(e, axis=0, keepdims=True))
        idx_rows.append(eidx)
    idx_ref[...] = (jnp.concatenate(idx_rows, axis=0).T * float(EXPERT_ROWS)).astype(jnp.int32)
    gate_ref[...] = jnp.concatenate(gate_rows, axis=0).T


def _out_call(x2d, oa, ob1, l1, ob4, l4, ob16, l16, oc, gn, wo, fn, wq, keys_p, *, batch, seq):
    n, d = x2d.shape
    tm = TOKEN_TILE
    nt = seq // tm
    bw = 3 * LANES
    row = lambda i: (i, 0)
    const2 = lambda i: (0, 0)
    bmap = lambda i: (i // nt, 0, i % nt, 0)
    n_e = PEER_HEADS * PEER_TOPK
    return pl.pallas_call(
        functools.partial(_out_kernel, tm=tm),
        name="out_peerq",
        grid=(n // tm,),
        in_specs=[
            pl.BlockSpec((tm, d), row),
            pl.BlockSpec((tm, bw), row),
            pl.BlockSpec((None, None, tm, bw), bmap), pl.BlockSpec((None, None, tm, bw), bmap),
            pl.BlockSpec((None, 4, tm // 4, bw), bmap), pl.BlockSpec((None, 4, tm // 4, bw), bmap),
            pl.BlockSpec((None, 16, tm // 16, bw), bmap), pl.BlockSpec((None, 16, tm // 16, bw), bmap),
            pl.BlockSpec((tm, 2 * LANES), row),
            pl.BlockSpec(gn.shape, const2),
            pl.BlockSpec(wo.shape, const2),
            pl.BlockSpec(fn.shape, const2),
            pl.BlockSpec(wq.shape, const2),
            pl.BlockSpec(keys_p.shape, lambda i: (0, 0, 0, 0)),
        ],
        out_specs=[pl.BlockSpec((tm, d), row), pl.BlockSpec((tm, d), row),
                   pl.BlockSpec((tm, n_e), row), pl.BlockSpec((tm, n_e), row)],
        out_shape=[jax.ShapeDtypeStruct((n, d), F32), jax.ShapeDtypeStruct((n, d), F32),
                   jax.ShapeDtypeStruct((n, n_e), jnp.int32), jax.ShapeDtypeStruct((n, n_e), F32)],
        scratch_shapes=[pltpu.VMEM((bw // LANES, tm, LANES), F32)] * 4,
        compiler_params=pltpu.CompilerParams(dimension_semantics=("arbitrary",), vmem_limit_bytes=VMEM_LIMIT),
    )(x2d, oa, ob1, l1, ob4, l4, ob16, l16, oc, gn, wo, fn, wq, keys_p)


def _idx_pipeline(idx_hbm, idx_smem, sem, nh, body):
    i = pl.program_id(0)
    n = pl.num_programs(0)

    def copy(block, slot):
        return pltpu.make_async_copy(idx_hbm.at[pl.ds(block * nh, nh)], idx_smem.at[slot], sem.at[slot])

    @pl.when(i == 0)
    def _():
        copy(0, 0).start()

    copy(2 * i, 0).wait()
    copy(2 * i + 1, 1).start()
    body(0, idx_smem.at[0])
    copy(2 * i + 1, 1).wait()

    @pl.when(i + 1 < n)
    def _():
        copy(2 * i + 2, 0).start()

    body(1, idx_smem.at[1])


N_ROT = 2
SUB_ROWS = 2 * EXPERT_ROWS


def _expert_row(tbl_ref, row):
    return tbl_ref[pl.ds(pl.multiple_of(row, EXPERT_ROWS), EXPERT_ROWS), :]


def _sub_row_pick(ne):
    sub = lax.broadcasted_iota(jnp.int32, (SUB_ROWS, SUB_ROWS * ne), 0)
    col = lax.broadcasted_iota(jnp.int32, (SUB_ROWS, SUB_ROWS * ne), 1)
    return (col % SUB_ROWS) == (2 * (sub % EXPERT_ROWS) + sub // EXPERT_ROWS)


def _stage_experts(tbl_ref, idx_ref, tt, st_ref, ne):
    for k in range(ne):
        st_ref[pl.ds(EXPERT_ROWS * k, EXPERT_ROWS), :] = _expert_row(tbl_ref, idx_ref[tt, k])
    return pltpu.bitcast(st_ref[...], BF16)


def _peer_u_kernel(idx_hbm, hn_ref, gate_ref, tbl_ref, e_ref, et_ref, aexp_ref, idx_smem, sem, da_ref, db_ref,
                   *stage_refs, nt, ne):
    nh = nt // 2
    pick = _sub_row_pick(ne)
    d_ref = (da_ref, db_ref)

    def finish(hf):
        d = d_ref[hf][...]
        hi = d.astype(BF16)
        r1 = d - hi.astype(F32)
        mid = r1.astype(BF16)
        lo = (r1 - mid.astype(F32)).astype(BF16)
        s = jnp.dot(jnp.concatenate([hi, mid, lo], axis=1), et_ref[...], preferred_element_type=F32)
        gelu = 0.5 * s * (1.0 + lax.erf(s * (2.0 ** -0.5)))
        act = (gelu * gate_ref[pl.ds(hf * nh, nh), :]).astype(BF16)
        aexp_ref[pl.ds(hf * nh, nh), :] = jnp.dot(act, e_ref[...], preferred_element_type=F32)

    def half(hf, idx_ref):
        for tt in range(nh):
            t = hf * nh + tt
            rows = _stage_experts(tbl_ref, idx_ref, tt, stage_refs[tt % N_ROT], ne)
            g = lax.dot_general(hn_ref[t].astype(BF16), rows, (((1,), (1,)), ((), ())), preferred_element_type=F32)
            d_ref[hf][pl.ds(tt, 1), :] = jnp.sum(jnp.where(pick, g, 0.0), axis=0, keepdims=True)
            if hf == 1 and tt == 0:
                finish(0)
        if hf == 1:
            finish(1)

    _idx_pipeline(idx_hbm, idx_smem, sem, nh, half)


def _peer_v_kernel(idx_hbm, aexp_ref, x_ref, tbl_ref, o_ref, idx_smem, sem, *stage_refs, nt, ne):
    nh = nt // 2
    pick = _sub_row_pick(ne)

    def half(hf, idx_ref):
        for tt in range(nh):
            t = hf * nh + tt
            rows = _stage_experts(tbl_ref, idx_ref, tt, stage_refs[tt % N_ROT], ne)
            a = jnp.where(pick, aexp_ref[pl.ds(t, 1), :], 0.0).astype(BF16)
            o_ref[t] = x_ref[t] + jnp.dot(a, rows, preferred_element_type=F32)

    _idx_pipeline(idx_hbm, idx_smem, sem, nh, half)


def _peer_call(x1, hn, idx, gate, tbl_u, tbl_v):
    n, d = x1.shape
    ne = idx.shape[1]
    nt = PEER_TOKENS
    rows = d // LANES
    row2 = lambda i: (i, 0)
    row3 = lambda i: (i, 0, 0)
    tbl_spec = pl.BlockSpec(tbl_u.shape, lambda i: (0, 0), pipeline_mode=pl.Buffered(1))
    idx_spec = pl.BlockSpec(memory_space=pl.ANY)
    idx_scratch = [pltpu.SMEM((2, nt // 2, ne), jnp.int32), pltpu.SemaphoreType.DMA((2,))]
    params = pltpu.CompilerParams(dimension_semantics=("arbitrary",), vmem_limit_bytes=PEER_VMEM_LIMIT)
    wide = SUB_ROWS * ne
    expand = np.kron(np.eye(ne), np.ones((1, SUB_ROWS)))
    stage = [pltpu.VMEM((EXPERT_ROWS * ne, LANES), jnp.uint32)] * N_ROT
    aexp = pl.pallas_call(
        functools.partial(_peer_u_kernel, nt=nt, ne=ne),
        name="peer_u",
        grid=(n // nt,),
        in_specs=[idx_spec, pl.BlockSpec((nt, rows, LANES), row3), pl.BlockSpec((nt, ne), row2), tbl_spec,
                  pl.BlockSpec((ne, wide), lambda i: (0, 0)), pl.BlockSpec((3 * wide, ne), lambda i: (0, 0))],
        out_specs=pl.BlockSpec((nt, wide), row2),
        out_shape=jax.ShapeDtypeStruct((n, wide), F32),
        scratch_shapes=idx_scratch + [pltpu.VMEM((nt // 2, wide), F32)] * 2 + stage,
        compiler_params=params,
    )(idx, hn.reshape(n, rows, LANES), gate, tbl_u, jnp.asarray(expand, dtype=BF16),
      jnp.asarray(np.tile(expand.T, (3, 1)), dtype=BF16))
    out = pl.pallas_call(
        functools.partial(_peer_v_kernel, nt=nt, ne=ne),
        name="peer_v",
        grid=(n // nt,),
        in_specs=[idx_spec, pl.BlockSpec((nt, wide), row2), pl.BlockSpec((nt, rows, LANES), row3), tbl_spec],
        out_specs=pl.BlockSpec((nt, rows, LANES), row3),
        out_shape=jax.ShapeDtypeStruct((n, rows, LANES), F32),
        scratch_shapes=idx_scratch + stage,
        compiler_params=params,
    )(idx, aexp, x1.reshape(n, rows, LANES), tbl_v)
    return out.reshape(n, d)


def _in_proj_perm():
    widths = (A_HEADS, A_KV_HEADS, A_KV_HEADS, B_HEADS, B_HEADS, B_HEADS, C_HEADS, C_KV_HEADS, C_KV_HEADS)
    orders = (A_Q_ORDER, None, None, None, None, None, C_Q_ORDER, None, None)
    cols, base = [], 0
    for w, order in zip(widths, orders):
        for hd in (order if order is not None else range(w)):
            cols.extend(range(base + hd * HEAD_DIM, base + (hd + 1) * HEAD_DIM))
        base += w * HEAD_DIM
    return np.asarray(cols, dtype=np.int32)


def _mix_perm():
    cols = []
    for hd in A_Q_ORDER:
        cols.extend(range(hd * HEAD_DIM, (hd + 1) * HEAD_DIM))
    base = A_HEADS * HEAD_DIM
    cols.extend(range(base, base + B_HEADS * HEAD_DIM))
    base += B_HEADS * HEAD_DIM
    for hd in C_Q_ORDER:
        cols.extend(range(base + hd * HEAD_DIM, base + (hd + 1) * HEAD_DIM))
    return np.asarray(cols, dtype=np.int32)


def _rope_tables(seq):
    pos = jnp.arange(seq, dtype=F32)
    f64 = ROPE_THETA ** (-jnp.arange(0, HEAD_DIM, 2, dtype=F32) / HEAD_DIM)
    ang = pos[:, None] * f64[None, :]
    cos1 = jnp.concatenate([jnp.cos(ang), jnp.cos(ang)], axis=-1)
    sin1 = jnp.concatenate([-jnp.sin(ang), jnp.sin(ang)], axis=-1)
    half = HEAD_DIM // 2
    f32_ = ROPE_THETA ** (-jnp.arange(0, half, 2, dtype=F32) / half)
    rows = seq // GRID_W
    row = jnp.repeat(jnp.arange(rows, dtype=F32), GRID_W)
    col = jnp.tile(jnp.arange(GRID_W, dtype=F32), rows)
    ar, ac = row[:, None] * f32_[None, :], col[:, None] * f32_[None, :]
    cos_a = jnp.concatenate([jnp.cos(ar), jnp.cos(ar), jnp.cos(ac), jnp.cos(ac)], axis=-1)
    sin_a = jnp.concatenate([-jnp.sin(ar), jnp.sin(ar), -jnp.sin(ac), jnp.sin(ac)], axis=-1)
    two = lambda t: jnp.concatenate([t, t], axis=-1)
    return jnp.stack([two(cos_a), two(cos1)]), jnp.stack([two(sin_a), two(sin1)])


def _group_gains(g):
    rows = []
    mix_id = {"a": 0, "b": 1, "c": 2}
    for mixer, role, kind in _GROUPS:
        if kind is None:
            rows.append(jnp.ones((LANES,), F32))
        else:
            gg = g[mix_id[mixer], 0 if role == "q" else 1].astype(F32)
            if role == "q":
                gg = gg * ATTN_SCALE
            rows.append(jnp.concatenate([gg, gg]))
    return jnp.stack(rows)


def _pack_table(t):
    e, d = t.shape
    assert d == 2 * EXPERT_ROWS * LANES
    b = lax.bitcast_convert_type(t.astype(BF16), jnp.uint16).astype(jnp.uint32)
    w = (b[:, d // 2:] << 16) | b[:, :d // 2]
    return w.reshape(e * EXPERT_ROWS, LANES)


def _pad_keys(keys):
    z = jnp.zeros_like(keys[:, 0])
    k0 = jnp.concatenate([keys[:, 0], z], axis=-1)
    k1 = jnp.concatenate([z, keys[:, 1]], axis=-1)
    return jnp.stack([k0, k1], axis=1).astype(BF16)


def _layer(x2d, batch, seq, g_attn, w_in, gains, cos_t, sin_t, bd, sink, g_out, w_out, g_ffn, wq, keys_p,
           tbl_u, tbl_v):
    n = x2d.shape[0]
    aq, ak, av, b1, b4, b16, cq, ck, cv = _proj_call(x2d, g_attn, w_in, bd, gains, cos_t, sin_t,
                                                     batch=batch, seq=seq)
    oa = _attn_call(
        "attn_a", aq, ak, av, (batch,),
        pl.BlockSpec((seq, 3 * LANES), lambda b: (b, 0)),
        pl.BlockSpec((seq, LANES), lambda b: (b, 0)),
        pl.BlockSpec((seq, LANES), lambda b: (b, 0)),
        jax.ShapeDtypeStruct((n, 3 * LANES), F32), pl.BlockSpec((seq, 3 * LANES), lambda b: (b, 0)),
        seq_len=seq, tq=256, win=seq, half=None, n_qp=3, n_kp=1)[0]
    ob, lse = [], []
    for (window, r), arr in zip(DILATIONS, (b1, b4, b16)):
        length = seq // r
        half = window // (2 * r)
        tq = min(128, length)
        win = min(tq + 2 * half, length)
        spec = lambda c: pl.BlockSpec((None, r, length, 3 * LANES), lambda b, c=c: (b, 0, 0, c))
        o_r, l_r = _attn_call(
            f"attn_b{r}", arr, arr, arr, (batch,), spec(0), spec(1), spec(2),
            jax.ShapeDtypeStruct((batch, r, length, 3 * LANES), F32), spec(0),
            seq_len=length, tq=tq, win=win, half=half, n_qp=3, n_kp=3, n_seq=r, emit_lse=True)
        ob.append(o_r)
        lse.append(l_r)
    tq = LOCAL_HALF_WINDOW
    oc = _attn_call(
        "attn_c", cq, ck, cv, (batch,),
        pl.BlockSpec((seq, 2 * LANES), lambda b: (b, 0)),
        pl.BlockSpec((seq, LANES), lambda b: (b, 0)),
        pl.BlockSpec((seq, LANES), lambda b: (b, 0)),
        jax.ShapeDtypeStruct((n, 2 * LANES), F32), pl.BlockSpec((seq, 2 * LANES), lambda b: (b, 0)),
        seq_len=seq, tq=tq, win=min(tq + 2 * LOCAL_HALF_WINDOW, seq), half=LOCAL_HALF_WINDOW,
        n_qp=2, n_kp=1, sink=sink)[0]
    x1, hn, idx, gate = _out_call(x2d, oa, ob[0], lse[0], ob[1], lse[1], ob[2], lse[2], oc,
                                  g_out, w_out, g_ffn, wq, keys_p, batch=batch, seq=seq)
    return _peer_call(x1, hn, idx, gate, tbl_u, tbl_v)


def kernel(x, attn_norm, w_in, qk_gain, sink_logits, out_norm, w_out, ffn_norm, peer_wq, peer_keys, peer_u, peer_v):
    batch, seq, d = x.shape
    depth = w_in.shape[0]
    in_perm = _in_proj_perm()
    mix_perm = _mix_perm()
    cos_t, sin_t = _rope_tables(seq)
    bd = jnp.asarray(np.kron(np.eye(2), np.ones((HEAD_DIM, HEAD_DIM))), dtype=BF16)
    sink_perm = np.asarray([C_Q_ORDER[2 * j + hf] for j in range(2) for hf in range(2)], dtype=np.int32)
    x2d = x.reshape(batch * seq, d)
    for layer in range(depth):
        x2d = _layer(
            x2d, batch, seq,
            attn_norm[layer][None, :].astype(F32),
            w_in[layer][:, in_perm].astype(BF16),
            _group_gains(qk_gain[layer]),
            cos_t, sin_t, bd,
            sink_logits[layer][sink_perm].astype(F32),
            out_norm[layer][mix_perm][None, :].astype(F32),
            w_out[layer][mix_perm, :].astype(BF16),
            ffn_norm[layer][None, :].astype(F32),
            peer_wq[layer].astype(BF16),
            _pad_keys(peer_keys[layer]),
            _pack_table(peer_u[layer]),
            _pack_table(peer_v[layer]),
        )
    return x2d.reshape(batch, seq, d)
```

```python
import functools

import numpy as np
import jax
import jax.numpy as jnp
from jax import lax
from jax.experimental import pallas as pl
from jax.experimental.pallas import tpu as pltpu

F32 = jnp.float32
BF16 = jnp.bfloat16

LANES = 128
HEAD_DIM = 64
A_HEADS, A_KV_HEADS = 6, 2
B_HEADS = 6
C_HEADS, C_KV_HEADS = 4, 2
ROPE_THETA = 10000.0
GRID_W = 64
DILATIONS = ((128, 1), (512, 4), (2048, 16))
LOCAL_HALF_WINDOW = 128
PEER_HEADS = 8
N_KEYS = 128
PEER_TOPK = 16
NORM_EPS = 1e-6
NEG_INF = -1e30
ATTN_SCALE = HEAD_DIM ** -0.5

TOKEN_TILE = 256
PEER_TOKENS = 64
EXPERT_ROWS = 4
ATTN_INTERLEAVE = 2
VMEM_LIMIT = 48 * 1024 * 1024
PEER_VMEM_LIMIT = 56 * 1024 * 1024

_GROUPS = ([("a", "q", "axial")] * 3 + [("a", "k", "axial")] + [("a", "v", None)]
           + [("b", "q", "1d")] * 3 + [("b", "k", "1d")] * 3 + [("b", "v", None)] * 3
           + [("c", "q", "1d")] * 2 + [("c", "k", "1d")] + [("c", "v", None)])
N_GROUPS = len(_GROUPS)
A_Q_ORDER = (0, 3, 1, 4, 2, 5)
C_Q_ORDER = (0, 2, 1, 3)


def _rms_rows(t):
    return t * lax.rsqrt(jnp.mean(t * t, axis=-1, keepdims=True) + NORM_EPS)


def _proj_kernel(x_ref, g_ref, w_ref, bd_ref, gain_ref, cos_ref, sin_ref,
                 aq_ref, ak_ref, av_ref, b1_ref, b4_ref, b16_ref, cq_ref, ck_ref, cv_ref,
                 scr_ref, *, tm):
    x = x_ref[...]
    h = _rms_rows(x) * g_ref[...]
    proj = jnp.dot(h.astype(BF16), w_ref[...], preferred_element_type=F32)
    bd = bd_ref[...]
    lane = lax.broadcasted_iota(jnp.int32, (1, LANES), 1)

    def norm_rope(y, j, kind):
        sq = y * y
        hi = sq.astype(BF16)
        lo = (sq - hi.astype(F32)).astype(BF16)
        ssq = (jnp.dot(hi, bd, preferred_element_type=F32)
               + jnp.dot(lo, bd, preferred_element_type=F32))
        yn = y * lax.rsqrt(ssq * (1.0 / HEAD_DIM) + NORM_EPS) * gain_ref[j:j + 1, :]
        t, sh = (0, 16) if kind == "axial" else (1, 32)
        first = (lane % (2 * sh)) < sh
        partner = jnp.where(first, pltpu.roll(yn, LANES - sh, 1), pltpu.roll(yn, sh, 1))
        return yn * cos_ref[t] + partner * sin_ref[t]

    outs = {("a", "q"): aq_ref, ("a", "k"): ak_ref, ("a", "v"): av_ref,
            ("c", "q"): cq_ref, ("c", "k"): ck_ref, ("c", "v"): cv_ref}
    pos = {}
    b_col = 0
    for j, (mixer, role, kind) in enumerate(_GROUPS):
        y = proj[:, j * LANES:(j + 1) * LANES]
        if kind is not None:
            y = norm_rope(y, j, kind)
        if mixer == "b":
            scr_ref[b_col] = y
            b1_ref[:, b_col * LANES:(b_col + 1) * LANES] = y.astype(BF16)
            b_col += 1
        else:
            c = pos.get((mixer, role), 0)
            outs[(mixer, role)][:, c * LANES:(c + 1) * LANES] = y.astype(BF16)
            pos[(mixer, role)] = c + 1
    for r, ref in ((4, b4_ref), (16, b16_ref)):
        for c in range(r):
            for g in range(b_col):
                ref[c, :, g * LANES:(g + 1) * LANES] = scr_ref[g, pl.ds(c, tm // r, stride=r), :].astype(BF16)


def _proj_call(x2d, g, w, bd, gains, cos_t, sin_t, *, batch, seq):
    n = x2d.shape[0]
    tm = TOKEN_TILE
    nt = seq // tm
    bw = 9 * LANES
    row = lambda i: (i, 0)
    const2 = lambda i: (0, 0)
    bmap = lambda i: (i // nt, 0, i % nt, 0)
    outs = [
        (jax.ShapeDtypeStruct((n, 3 * LANES), BF16), pl.BlockSpec((tm, 3 * LANES), row)),
        (jax.ShapeDtypeStruct((n, LANES), BF16), pl.BlockSpec((tm, LANES), row)),
        (jax.ShapeDtypeStruct((n, LANES), BF16), pl.BlockSpec((tm, LANES), row)),
        (jax.ShapeDtypeStruct((batch, 1, seq, bw), BF16), pl.BlockSpec((None, None, tm, bw), bmap)),
        (jax.ShapeDtypeStruct((batch, 4, seq // 4, bw), BF16), pl.BlockSpec((None, 4, tm // 4, bw), bmap)),
        (jax.ShapeDtypeStruct((batch, 16, seq // 16, bw), BF16), pl.BlockSpec((None, 16, tm // 16, bw), bmap)),
        (jax.ShapeDtypeStruct((n, 2 * LANES), BF16), pl.BlockSpec((tm, 2 * LANES), row)),
        (jax.ShapeDtypeStruct((n, LANES), BF16), pl.BlockSpec((tm, LANES), row)),
        (jax.ShapeDtypeStruct((n, LANES), BF16), pl.BlockSpec((tm, LANES), row)),
    ]
    return pl.pallas_call(
        functools.partial(_proj_kernel, tm=tm),
        name="proj",
        grid=(n // tm,),
        in_specs=[
            pl.BlockSpec((tm, x2d.shape[1]), row),
            pl.BlockSpec(g.shape, const2),
            pl.BlockSpec(w.shape, const2),
            pl.BlockSpec(bd.shape, const2),
            pl.BlockSpec(gains.shape, const2),
            pl.BlockSpec((2, tm, LANES), lambda i: (0, i % nt, 0)),
            pl.BlockSpec((2, tm, LANES), lambda i: (0, i % nt, 0)),
        ],
        out_specs=[o[1] for o in outs],
        out_shape=[o[0] for o in outs],
        scratch_shapes=[pltpu.VMEM((bw // LANES, tm, LANES), F32)],
        compiler_params=pltpu.CompilerParams(dimension_semantics=("arbitrary",), vmem_limit_bytes=VMEM_LIMIT),
    )(x2d, g, w, bd, gains, cos_t, sin_t)


def _attn_kernel(*refs, n_seq, seq_len, tq, win, half, n_qp, n_kp, has_sink, emit_lse):
    it = iter(refs)
    q_all, k_all, v_all = next(it), next(it), next(it)
    sink_ref = next(it) if has_sink else None
    o_all = next(it)
    lse_all = next(it) if emit_lse else None
    lane = lax.broadcasted_iota(jnp.int32, (1, LANES), 1)
    low = lane < HEAD_DIM
    n_tiles = seq_len // tq

    total = max(n_seq, 1) * n_tiles
    staged = half is not None
    group = ATTN_INTERLEAVE if (staged and total % ATTN_INTERLEAVE == 0) else 1

    def tile_group(g, carry):
        for u in range(group):
            one_tile(g * group + u)
        return carry

    def one_tile(i):
        if n_seq:
            sq = i // n_tiles
            q_ref, k_ref, v_ref, o_ref = q_all.at[sq], k_all.at[sq], v_all.at[sq], o_all.at[sq]
            lse_ref = lse_all.at[sq] if emit_lse else None
            r0 = pl.multiple_of((i % n_tiles) * tq, tq)
        else:
            q_ref, k_ref, v_ref, o_ref, lse_ref = q_all, k_all, v_all, o_all, lse_all
            r0 = pl.multiple_of(i * tq, tq)
        if half is None or win == seq_len:
            ks = 0
        else:
            ks = pl.multiple_of(jnp.clip(r0 - half, 0, seq_len - win), half)
        if half is not None:
            d = (r0 - ks) + lax.broadcasted_iota(jnp.int32, (tq, win), 0) - lax.broadcasted_iota(jnp.int32, (tq, win), 1)
            visible = jnp.abs(d) <= half
        s_h, v_j = [], []
        for j in range(n_qp):
            kj = j if n_kp > 1 else 0
            q = q_ref[pl.ds(r0, tq), j * LANES:(j + 1) * LANES]
            k = k_ref[pl.ds(ks, win), kj * LANES:(kj + 1) * LANES]
            v_j.append(v_ref[pl.ds(ks, win), kj * LANES:(kj + 1) * LANES])
            for hf in range(2):
                qm = jnp.where(low if hf == 0 else jnp.logical_not(low), q, jnp.zeros_like(q))
                s = lax.dot_general(qm, k, (((1,), (1,)), ((), ())), preferred_element_type=F32)
                s_h.append(jnp.where(visible, s, NEG_INF) if half is not None else s)
        n_h = len(s_h)
        stages = [list(range(n_h))] if staged else [[h] for h in range(n_h)]
        o_h, lse_h = [None] * n_h, [None] * n_h
        for hs in stages:
            m_all = jnp.max(jnp.concatenate([s_h[h] for h in hs], axis=0), axis=-1, keepdims=True)
            m_h, p_h = {}, {}
            for n, h in enumerate(hs):
                m = m_all[n * tq:(n + 1) * tq]
                m_h[h] = jnp.maximum(m, sink_ref[h]) if has_sink else m
                p_h[h] = jnp.exp(s_h[h] - m_h[h])
            l_all = jnp.sum(jnp.concatenate([p_h[h] for h in hs], axis=0), axis=-1, keepdims=True)
            for n, h in enumerate(hs):
                l = l_all[n * tq:(n + 1) * tq]
                if has_sink:
                    l = l + jnp.exp(sink_ref[h] - m_h[h])
                o = jnp.dot(p_h[h].astype(BF16), v_j[h // 2], preferred_element_type=F32)
                o_h[h] = o / l
                if emit_lse:
                    lse_h[h] = m_h[h] + jnp.log(l)
        for j in range(n_qp):
            o_ref[pl.ds(r0, tq), j * LANES:(j + 1) * LANES] = jnp.where(low, o_h[2 * j], o_h[2 * j + 1])
            if emit_lse:
                lse_ref[pl.ds(r0, tq), j * LANES:(j + 1) * LANES] = jnp.where(low, lse_h[2 * j], lse_h[2 * j + 1])

    lax.fori_loop(0, total // group, tile_group, 0)


def _attn_call(name, q_arg, k_arg, v_arg, grid, q_spec, k_spec, v_spec, out_shape, out_spec, *,
               seq_len, tq, win, half, n_qp, n_kp, n_seq=0, sink=None, emit_lse=False):
    in_specs = [q_spec, k_spec, v_spec]
    args = [q_arg, k_arg, v_arg]
    if sink is not None:
        in_specs.append(pl.BlockSpec(memory_space=pltpu.SMEM))
        args.append(sink)
    n_out = 2 if emit_lse else 1
    res = pl.pallas_call(
        functools.partial(_attn_kernel, n_seq=n_seq, seq_len=seq_len, tq=tq, win=win, half=half, n_qp=n_qp, n_kp=n_kp,
                          has_sink=sink is not None, emit_lse=emit_lse),
        name=name,
        grid=grid,
        in_specs=in_specs,
        out_specs=[out_spec] * n_out,
        out_shape=[out_shape] * n_out,
        compiler_params=pltpu.CompilerParams(dimension_semantics=("arbitrary",) * len(grid),
                                             vmem_limit_bytes=VMEM_LIMIT),
    )(*args)
    return res


def _top_rows(s, payload=None):
    n_rows = s.shape[0]
    riota = lax.broadcasted_iota(jnp.int32, s.shape, 0).astype(F32)
    vals, picks = [], []
    for _ in range(PEER_TOPK):
        m = jnp.max(s, axis=0, keepdims=True)
        am = jnp.min(jnp.where(s == m, riota, float(n_rows)), axis=0, keepdims=True)
        hit = riota == am
        vals.append(m)
        if payload is None:
            picks.append(am)
        else:
            picks.append(jnp.max(jnp.where(hit, payload, -1.0), axis=0, keepdims=True))
        s = jnp.where(hit, -jnp.inf, s)
    return jnp.concatenate(vals, axis=0), jnp.concatenate(picks, axis=0)


def _pair_candidates(t0, t1, combine):
    h = PEER_TOPK // 2
    blocks = [combine(t0[0:1], t1)]
    blocks += [combine(t0[a:a + 1], t1[0:h]) for a in range(1, h)]
    blocks.append(combine(t0[h:], t1[0:1]))
    return jnp.concatenate(blocks, axis=0)


def _out_kernel(x_ref, oa_ref, ob1_ref, l1_ref, ob4_ref, l4_ref, ob16_ref, l16_ref, oc_ref,
                gn_ref, wo_ref, fn_ref, wq_ref, keys_ref,
                x1_ref, hn_ref, idx_ref, gate_ref,
                so4, sl4, so16, sl16, *, tm):
    ng = ob1_ref.shape[1] // LANES
    for r, o_src, l_src, o_dst, l_dst in ((4, ob4_ref, l4_ref, so4, sl4), (16, ob16_ref, l16_ref, so16, sl16)):
        for c in range(r):
            for g in range(ng):
                o_dst[g, pl.ds(c, tm // r, stride=r), :] = o_src[c, :, g * LANES:(g + 1) * LANES]
                l_dst[g, pl.ds(c, tm // r, stride=r), :] = l_src[c, :, g * LANES:(g + 1) * LANES]
    cat = lambda ref: jnp.concatenate([ref[g] for g in range(ng)], axis=-1)
    l1, l4, l16 = l1_ref[...], cat(sl4), cat(sl16)
    mx = jnp.maximum(jnp.maximum(l1, l4), l16)
    e1, e4, e16 = jnp.exp(l1 - mx), jnp.exp(l4 - mx), jnp.exp(l16 - mx)
    ob = (e1 * ob1_ref[...] + e4 * cat(so4) + e16 * cat(so16)) / (e1 + e4 + e16)
    mix = jnp.concatenate([_rms_rows(oa_ref[...]), _rms_rows(ob), _rms_rows(oc_ref[...])], axis=-1) * gn_ref[...]
    x1 = x_ref[...] + jnp.dot(mix.astype(BF16), wo_ref[...], preferred_element_type=F32)
    x1_ref[...] = x1
    hn = _rms_rows(x1) * fn_ref[...]
    hn_ref[...] = hn
    qp = jnp.dot(hn.astype(BF16), wq_ref[...], preferred_element_type=F32)

    idx_rows, gate_rows = [], []
    for h in range(PEER_HEADS):
        qh = qp[:, h * LANES:(h + 1) * LANES].astype(BF16)
        sv, si = [], []
        for p in range(2):
            st = lax.dot_general(keys_ref[h, p], qh, (((1,), (1,)), ((), ())), preferred_element_type=F32)
            v, i = _top_rows(st)
            sv.append(v)
            si.append(i)
        cand = _pair_candidates(sv[0], sv[1], lambda a, b: a + b)
        cidx = _pair_candidates(si[0], si[1], lambda a, b: a * float(N_KEYS) + b)
        best, eidx = _top_rows(cand, payload=cidx)
        e = jnp.exp(best - best[0:1])
        gate_rows.append(e / jnp.sum(e, axis=0, keepdims=True))
        idx_rows.append(eidx)
    idx_ref[...] = (jnp.concatenate(idx_rows, axis=0).T * float(EXPERT_ROWS)).astype(jnp.int32)
    gate_ref[...] = jnp.concatenate(gate_rows, axis=0).T


def _out_call(x2d, oa, ob1, l1, ob4, l4, ob16, l16, oc, gn, wo, fn, wq, keys_p, *, batch, seq):
    n, d = x2d.shape
    tm = TOKEN_TILE
    nt = seq // tm
    bw = 3 * LANES
    row = lambda i: (i, 0)
    const2 = lambda i: (0, 0)
    bmap = lambda i: (i // nt, 0, i % nt, 0)
    n_e = PEER_HEADS * PEER_TOPK
    return pl.pallas_call(
        functools.partial(_out_kernel, tm=tm),
        name="out_peerq",
        grid=(n // tm,),
        in_specs=[
            pl.BlockSpec((tm, d), row),
            pl.BlockSpec((tm, bw), row),
            pl.BlockSpec((None, None, tm, bw), bmap), pl.BlockSpec((None, None, tm, bw), bmap),
            pl.BlockSpec((None, 4, tm // 4, bw), bmap), pl.BlockSpec((None, 4, tm // 4, bw), bmap),
            pl.BlockSpec((None, 16, tm // 16, bw), bmap), pl.BlockSpec((None, 16, tm // 16, bw), bmap),
            pl.BlockSpec((tm, 2 * LANES), row),
            pl.BlockSpec(gn.shape, const2),
            pl.BlockSpec(wo.shape, const2),
            pl.BlockSpec(fn.shape, const2),
            pl.BlockSpec(wq.shape, const2),
            pl.BlockSpec(keys_p.shape, lambda i: (0, 0, 0, 0)),
        ],
        out_specs=[pl.BlockSpec((tm, d), row), pl.BlockSpec((tm, d), row),
                   pl.BlockSpec((tm, n_e), row), pl.BlockSpec((tm, n_e), row)],
        out_shape=[jax.ShapeDtypeStruct((n, d), F32), jax.ShapeDtypeStruct((n, d), F32),
                   jax.ShapeDtypeStruct((n, n_e), jnp.int32), jax.ShapeDtypeStruct((n, n_e), F32)],
        scratch_shapes=[pltpu.VMEM((bw // LANES, tm, LANES), F32)] * 4,
        compiler_params=pltpu.CompilerParams(dimension_semantics=("arbitrary",), vmem_limit_bytes=VMEM_LIMIT),
    )(x2d, oa, ob1, l1, ob4, l4, ob16, l16, oc, gn, wo, fn, wq, keys_p)


def _idx_pipeline(idx_hbm, idx_smem, sem, nh, body):
    i = pl.program_id(0)
    n = pl.num_programs(0)

    def copy(block, slot):
        return pltpu.make_async_copy(idx_hbm.at[pl.ds(block * nh, nh)], idx_smem.at[slot], sem.at[slot])

    @pl.when(i == 0)
    def _():
        copy(0, 0).start()

    copy(2 * i, 0).wait()
    copy(2 * i + 1, 1).start()
    body(0, idx_smem.at[0])
    copy(2 * i + 1, 1).wait()

    @pl.when(i + 1 < n)
    def _():
        copy(2 * i + 2, 0).start()

    body(1, idx_smem.at[1])


N_ROT = 2
SUB_ROWS = 2 * EXPERT_ROWS


def _expert_row(tbl_ref, row):
    return tbl_ref[pl.ds(pl.multiple_of(row, EXPERT_ROWS), EXPERT_ROWS), :]


def _sub_row_pick(ne):
    sub = lax.broadcasted_iota(jnp.int32, (SUB_ROWS, SUB_ROWS * ne), 0)
    col = lax.broadcasted_iota(jnp.int32, (SUB_ROWS, SUB_ROWS * ne), 1)
    return (col % SUB_ROWS) == (2 * (sub % EXPERT_ROWS) + sub // EXPERT_ROWS)


def _stage_experts(tbl_ref, idx_ref, tt, st_ref, ne):
    for k in range(ne):
        st_ref[pl.ds(EXPERT_ROWS * k, EXPERT_ROWS), :] = _expert_row(tbl_ref, idx_ref[tt, k])
    return pltpu.bitcast(st_ref[...], BF16)


def _peer_u_kernel(idx_hbm, hn_ref, gate_ref, tbl_ref, e_ref, et_ref, aexp_ref, idx_smem, sem, da_ref, db_ref,
                   *stage_refs, nt, ne):
    nh = nt // 2
    pick = _sub_row_pick(ne)
    d_ref = (da_ref, db_ref)

    def finish(hf):
        d = d_ref[hf][...]
        hi = d.astype(BF16)
        r1 = d - hi.astype(F32)
        mid = r1.astype(BF16)
        lo = (r1 - mid.astype(F32)).astype(BF16)
        s = jnp.dot(jnp.concatenate([hi, mid, lo], axis=1), et_ref[...], preferred_element_type=F32)
        gelu = 0.5 * s * (1.0 + lax.erf(s * (2.0 ** -0.5)))
        act = (gelu * gate_ref[pl.ds(hf * nh, nh), :]).astype(BF16)
        aexp_ref[pl.ds(hf * nh, nh), :] = jnp.dot(act, e_ref[...], preferred_element_type=F32)

    def half(hf, idx_ref):
        for tt in range(nh):
            t = hf * nh + tt
            rows = _stage_experts(tbl_ref, idx_ref, tt, stage_refs[tt % N_ROT], ne)
            x8 = hn_ref[pl.ds(t, 1), :].reshape(SUB_ROWS, LANES).astype(BF16)
            g = lax.dot_general(x8, rows, (((1,), (1,)), ((), ())), preferred_element_type=F32)
            d_ref[hf][pl.ds(tt, 1), :] = jnp.sum(jnp.where(pick, g, 0.0), axis=0, keepdims=True)
            if hf == 1 and tt == 0:
                finish(0)
        if hf == 1:
            finish(1)

    _idx_pipeline(idx_hbm, idx_smem, sem, nh, half)


def _peer_v_kernel(idx_hbm, aexp_ref, x_ref, tbl_ref, o_ref, idx_smem, sem, *stage_refs, nt, ne):
    nh = nt // 2
    pick = _sub_row_pick(ne)

    def half(hf, idx_ref):
        for tt in range(nh):
            t = hf * nh + tt
            rows = _stage_experts(tbl_ref, idx_ref, tt, stage_refs[tt % N_ROT], ne)
            a = jnp.where(pick, aexp_ref[pl.ds(t, 1), :], 0.0).astype(BF16)
            upd = jnp.dot(a, rows, preferred_element_type=F32)
            o_ref[pl.ds(t, 1), :] = x_ref[pl.ds(t, 1), :] + upd.reshape(1, SUB_ROWS * LANES)

    _idx_pipeline(idx_hbm, idx_smem, sem, nh, half)


def _peer_call(x1, hn, idx, gate, tbl_u, tbl_v):
    n, d = x1.shape
    ne = idx.shape[1]
    nt = PEER_TOKENS
    assert d == SUB_ROWS * LANES
    row2 = lambda i: (i, 0)
    tbl_spec = pl.BlockSpec(tbl_u.shape, lambda i: (0, 0), pipeline_mode=pl.Buffered(1))
    idx_spec = pl.BlockSpec(memory_space=pl.ANY)
    idx_scratch = [pltpu.SMEM((2, nt // 2, ne), jnp.int32), pltpu.SemaphoreType.DMA((2,))]
    params = pltpu.CompilerParams(dimension_semantics=("arbitrary",), vmem_limit_bytes=PEER_VMEM_LIMIT)
    wide = SUB_ROWS * ne
    expand = np.kron(np.eye(ne), np.ones((1, SUB_ROWS)))
    stage = [pltpu.VMEM((EXPERT_ROWS * ne, LANES), jnp.uint32)] * N_ROT
    aexp = pl.pallas_call(
        functools.partial(_peer_u_kernel, nt=nt, ne=ne),
        name="peer_u",
        grid=(n // nt,),
        in_specs=[idx_spec, pl.BlockSpec((nt, d), row2), pl.BlockSpec((nt, ne), row2), tbl_spec,
                  pl.BlockSpec((ne, wide), lambda i: (0, 0)), pl.BlockSpec((3 * wide, ne), lambda i: (0, 0))],
        out_specs=pl.BlockSpec((nt, wide), row2),
        out_shape=jax.ShapeDtypeStruct((n, wide), F32),
        scratch_shapes=idx_scratch + [pltpu.VMEM((nt // 2, wide), F32)] * 2 + stage,
        compiler_params=params,
    )(idx, hn, gate, tbl_u, jnp.asarray(expand, dtype=BF16),
      jnp.asarray(np.tile(expand.T, (3, 1)), dtype=BF16))
    out = pl.pallas_call(
        functools.partial(_peer_v_kernel, nt=nt, ne=ne),
        name="peer_v",
        grid=(n // nt,),
        in_specs=[idx_spec, pl.BlockSpec((nt, wide), row2), pl.BlockSpec((nt, d), row2), tbl_spec],
        out_specs=pl.BlockSpec((nt, d), row2),
        out_shape=jax.ShapeDtypeStruct((n, d), F32),
        scratch_shapes=idx_scratch + stage,
        compiler_params=params,
    )(idx, aexp, x1, tbl_v)
    return out


def _in_proj_perm():
    widths = (A_HEADS, A_KV_HEADS, A_KV_HEADS, B_HEADS, B_HEADS, B_HEADS, C_HEADS, C_KV_HEADS, C_KV_HEADS)
    orders = (A_Q_ORDER, None, None, None, None, None, C_Q_ORDER, None, None)
    cols, base = [], 0
    for w, order in zip(widths, orders):
        for hd in (order if order is not None else range(w)):
            cols.extend(range(base + hd * HEAD_DIM, base + (hd + 1) * HEAD_DIM))
        base += w * HEAD_DIM
    return np.asarray(cols, dtype=np.int32)


def _mix_perm():
    cols = []
    for hd in A_Q_ORDER:
        cols.extend(range(hd * HEAD_DIM, (hd + 1) * HEAD_DIM))
    base = A_HEADS * HEAD_DIM
    cols.extend(range(base, base + B_HEADS * HEAD_DIM))
    base += B_HEADS * HEAD_DIM
    for hd in C_Q_ORDER:
        cols.extend(range(base + hd * HEAD_DIM, base + (hd + 1) * HEAD_DIM))
    return np.asarray(cols, dtype=np.int32)


def _rope_tables(seq):
    pos = jnp.arange(seq, dtype=F32)
    f64 = ROPE_THETA ** (-jnp.arange(0, HEAD_DIM, 2, dtype=F32) / HEAD_DIM)
    ang = pos[:, None] * f64[None, :]
    cos1 = jnp.concatenate([jnp.cos(ang), jnp.cos(ang)], axis=-1)
    sin1 = jnp.concatenate([-jnp.sin(ang), jnp.sin(ang)], axis=-1)
    half = HEAD_DIM // 2
    f32_ = ROPE_THETA ** (-jnp.arange(0, half, 2, dtype=F32) / half)
    rows = seq // GRID_W
    row = jnp.repeat(jnp.arange(rows, dtype=F32), GRID_W)
    col = jnp.tile(jnp.arange(GRID_W, dtype=F32), rows)
    ar, ac = row[:, None] * f32_[None, :], col[:, None] * f32_[None, :]
    cos_a = jnp.concatenate([jnp.cos(ar), jnp.cos(ar), jnp.cos(ac), jnp.cos(ac)], axis=-1)
    sin_a = jnp.concatenate([-jnp.sin(ar), jnp.sin(ar), -jnp.sin(ac), jnp.sin(ac)], axis=-1)
    two = lambda t: jnp.concatenate([t, t], axis=-1)
    return jnp.stack([two(cos_a), two(cos1)]), jnp.stack([two(sin_a), two(sin1)])


def _group_gains(g):
    rows = []
    mix_id = {"a": 0, "b": 1, "c": 2}
    for mixer, role, kind in _GROUPS:
        if kind is None:
            rows.append(jnp.ones((LANES,), F32))
        else:
            gg = g[mix_id[mixer], 0 if role == "q" else 1].astype(F32)
            if role == "q":
                gg = gg * ATTN_SCALE
            rows.append(jnp.concatenate([gg, gg]))
    return jnp.stack(rows)


def _pack_table(t):
    e, d = t.shape
    assert d == 2 * EXPERT_ROWS * LANES
    b = lax.bitcast_convert_type(t.astype(BF16), jnp.uint16).astype(jnp.uint32)
    w = (b[:, d // 2:] << 16) | b[:, :d // 2]
    return w.reshape(e * EXPERT_ROWS, LANES)


def _pad_keys(keys):
    z = jnp.zeros_like(keys[:, 0])
    k0 = jnp.concatenate([keys[:, 0], z], axis=-1)
    k1 = jnp.concatenate([z, keys[:, 1]], axis=-1)
    return jnp.stack([k0, k1], axis=1).astype(BF16)


def _layer(x2d, batch, seq, g_attn, w_in, gains, cos_t, sin_t, bd, sink, g_out, w_out, g_ffn, wq, keys_p,
           tbl_u, tbl_v):
    n = x2d.shape[0]
    aq, ak, av, b1, b4, b16, cq, ck, cv = _proj_call(x2d, g_attn, w_in, bd, gains, cos_t, sin_t,
                                                     batch=batch, seq=seq)
    oa = _attn_call(
        "attn_a", aq, ak, av, (batch,),
        pl.BlockSpec((seq, 3 * LANES), lambda b: (b, 0)),
        pl.BlockSpec((seq, LANES), lambda b: (b, 0)),
        pl.BlockSpec((seq, LANES), lambda b: (b, 0)),
        jax.ShapeDtypeStruct((n, 3 * LANES), F32), pl.BlockSpec((seq, 3 * LANES), lambda b: (b, 0)),
        seq_len=seq, tq=256, win=seq, half=None, n_qp=3, n_kp=1)[0]
    ob, lse = [], []
    for (window, r), arr in zip(DILATIONS, (b1, b4, b16)):
        length = seq // r
        half = window // (2 * r)
        tq = min(128, length)
        win = min(tq + 2 * half, length)
        spec = lambda c: pl.BlockSpec((None, r, length, 3 * LANES), lambda b, c=c: (b, 0, 0, c))
        o_r, l_r = _attn_call(
            f"attn_b{r}", arr, arr, arr, (batch,), spec(0), spec(1), spec(2),
            jax.ShapeDtypeStruct((batch, r, length, 3 * LANES), F32), spec(0),
            seq_len=length, tq=tq, win=win, half=half, n_qp=3, n_kp=3, n_seq=r, emit_lse=True)
        ob.append(o_r)
        lse.append(l_r)
    tq = LOCAL_HALF_WINDOW
    oc = _attn_call(
        "attn_c", cq, ck, cv, (batch,),
        pl.BlockSpec((seq, 2 * LANES), lambda b: (b, 0)),
        pl.BlockSpec((seq, LANES), lambda b: (b, 0)),
        pl.BlockSpec((seq, LANES), lambda b: (b, 0)),
        jax.ShapeDtypeStruct((n, 2 * LANES), F32), pl.BlockSpec((seq, 2 * LANES), lambda b: (b, 0)),
        seq_len=seq, tq=tq, win=min(tq + 2 * LOCAL_HALF_WINDOW, seq), half=LOCAL_HALF_WINDOW,
        n_qp=2, n_kp=1, sink=sink)[0]
    x1, hn, idx, gate = _out_call(x2d, oa, ob[0], lse[0], ob[1], lse[1], ob[2], lse[2], oc,
                                  g_out, w_out, g_ffn, wq, keys_p, batch=batch, seq=seq)
    return _peer_call(x1, hn, idx, gate, tbl_u, tbl_v)


def kernel(x, attn_norm, w_in, qk_gain, sink_logits, out_norm, w_out, ffn_norm, peer_wq, peer_keys, peer_u, peer_v):
    batch, seq, d = x.shape
    depth = w_in.shape[0]
    in_perm = _in_proj_perm()
    mix_perm = _mix_perm()
    cos_t, sin_t = _rope_tables(seq)
    bd = jnp.asarray(np.kron(np.eye(2), np.ones((HEAD_DIM, HEAD_DIM))), dtype=BF16)
    sink_perm = np.asarray([C_Q_ORDER[2 * j + hf] for j in range(2) for hf in range(2)], dtype=np.int32)
    x2d = x.reshape(batch * seq, d)
    for layer in range(depth):
        x2d = _layer(
            x2d, batch, seq,
            attn_norm[layer][None, :].astype(F32),
            w_in[layer][:, in_perm].astype(BF16),
            _group_gains(qk_gain[layer]),
            cos_t, sin_t, bd,
            sink_logits[layer][sink_perm].astype(F32),
            out_norm[layer][mix_perm][None, :].astype(F32),
            w_out[layer][mix_perm, :].astype(BF16),
            ffn_norm[layer][None, :].astype(F32),
            peer_wq[layer].astype(BF16),
            _pad_keys(peer_keys[layer]),
            _pack_table(peer_u[layer]),
            _pack_table(peer_v[layer]),
        )
    return x2d.reshape(batch, seq, d)
```

```python
import functools

import numpy as np
import jax
import jax.numpy as jnp
from jax import lax
from jax.experimental import pallas as pl
from jax.experimental.pallas import tpu as pltpu

F32 = jnp.float32
BF16 = jnp.bfloat16

LANES = 128
HEAD_DIM = 64
A_HEADS, A_KV_HEADS = 6, 2
B_HEADS = 6
C_HEADS, C_KV_HEADS = 4, 2
ROPE_THETA = 10000.0
GRID_W = 64
DILATIONS = ((128, 1), (512, 4), (2048, 16))
LOCAL_HALF_WINDOW = 128
PEER_HEADS = 8
N_KEYS = 128
PEER_TOPK = 16
NORM_EPS = 1e-6
NEG_INF = -1e30
ATTN_SCALE = HEAD_DIM ** -0.5

TOKEN_TILE = 256
PEER_TOKENS = 64
EXPERT_ROWS = 4
ATTN_INTERLEAVE = 2
VMEM_LIMIT = 48 * 1024 * 1024
PEER_VMEM_LIMIT = 56 * 1024 * 1024

_GROUPS = ([("a", "q", "axial")] * 3 + [("a", "k", "axial")] + [("a", "v", None)]
           + [("b", "q", "1d")] * 3 + [("b", "k", "1d")] * 3 + [("b", "v", None)] * 3
           + [("c", "q", "1d")] * 2 + [("c", "k", "1d")] + [("c", "v", None)])
N_GROUPS = len(_GROUPS)
A_Q_ORDER = (0, 3, 1, 4, 2, 5)
C_Q_ORDER = (0, 2, 1, 3)


def _rms_rows(t):
    return t * lax.rsqrt(jnp.mean(t * t, axis=-1, keepdims=True) + NORM_EPS)


def _proj_kernel(x_ref, g_ref, w_ref, bd_ref, gain_ref, cos_ref, sin_ref,
                 aq_ref, ak_ref, av_ref, b1_ref, b4_ref, b16_ref, cq_ref, ck_ref, cv_ref,
                 scr_ref, *, tm):
    x = x_ref[...]
    h = _rms_rows(x) * g_ref[...]
    proj = jnp.dot(h.astype(BF16), w_ref[...], preferred_element_type=F32)
    bd = bd_ref[...]
    lane = lax.broadcasted_iota(jnp.int32, (1, LANES), 1)

    def norm_rope(y, j, kind):
        sq = y * y
        hi = sq.astype(BF16)
        lo = (sq - hi.astype(F32)).astype(BF16)
        ssq = (jnp.dot(hi, bd, preferred_element_type=F32)
               + jnp.dot(lo, bd, preferred_element_type=F32))
        yn = y * lax.rsqrt(ssq * (1.0 / HEAD_DIM) + NORM_EPS) * gain_ref[j:j + 1, :]
        t, sh = (0, 16) if kind == "axial" else (1, 32)
        first = (lane % (2 * sh)) < sh
        partner = jnp.where(first, pltpu.roll(yn, LANES - sh, 1), pltpu.roll(yn, sh, 1))
        return yn * cos_ref[t] + partner * sin_ref[t]

    outs = {("a", "q"): aq_ref, ("a", "k"): ak_ref, ("a", "v"): av_ref,
            ("c", "q"): cq_ref, ("c", "k"): ck_ref, ("c", "v"): cv_ref}
    pos = {}
    b_col = 0
    for j, (mixer, role, kind) in enumerate(_GROUPS):
        y = proj[:, j * LANES:(j + 1) * LANES]
        if kind is not None:
            y = norm_rope(y, j, kind)
        if mixer == "b":
            scr_ref[b_col] = y
            b1_ref[:, b_col * LANES:(b_col + 1) * LANES] = y.astype(BF16)
            b_col += 1
        else:
            c = pos.get((mixer, role), 0)
            outs[(mixer, role)][:, c * LANES:(c + 1) * LANES] = y.astype(BF16)
            pos[(mixer, role)] = c + 1
    for r, ref in ((4, b4_ref), (16, b16_ref)):
        for c in range(r):
            for g in range(b_col):
                ref[c, :, g * LANES:(g + 1) * LANES] = scr_ref[g, pl.ds(c, tm // r, stride=r), :].astype(BF16)


def _proj_call(x2d, g, w, bd, gains, cos_t, sin_t, *, batch, seq):
    n = x2d.shape[0]
    tm = TOKEN_TILE
    nt = seq // tm
    bw = 9 * LANES
    row = lambda i: (i, 0)
    const2 = lambda i: (0, 0)
    bmap = lambda i: (i // nt, 0, i % nt, 0)
    outs = [
        (jax.ShapeDtypeStruct((n, 3 * LANES), BF16), pl.BlockSpec((tm, 3 * LANES), row)),
        (jax.ShapeDtypeStruct((n, LANES), BF16), pl.BlockSpec((tm, LANES), row)),
        (jax.ShapeDtypeStruct((n, LANES), BF16), pl.BlockSpec((tm, LANES), row)),
        (jax.ShapeDtypeStruct((batch, 1, seq, bw), BF16), pl.BlockSpec((None, None, tm, bw), bmap)),
        (jax.ShapeDtypeStruct((batch, 4, seq // 4, bw), BF16), pl.BlockSpec((None, 4, tm // 4, bw), bmap)),
        (jax.ShapeDtypeStruct((batch, 16, seq // 16, bw), BF16), pl.BlockSpec((None, 16, tm // 16, bw), bmap)),
        (jax.ShapeDtypeStruct((n, 2 * LANES), BF16), pl.BlockSpec((tm, 2 * LANES), row)),
        (jax.ShapeDtypeStruct((n, LANES), BF16), pl.BlockSpec((tm, LANES), row)),
        (jax.ShapeDtypeStruct((n, LANES), BF16), pl.BlockSpec((tm, LANES), row)),
    ]
    return pl.pallas_call(
        functools.partial(_proj_kernel, tm=tm),
        name="proj",
        grid=(n // tm,),
        in_specs=[
            pl.BlockSpec((tm, x2d.shape[1]), row),
            pl.BlockSpec(g.shape, const2),
            pl.BlockSpec(w.shape, const2),
            pl.BlockSpec(bd.shape, const2),
            pl.BlockSpec(gains.shape, const2),
            pl.BlockSpec((2, tm, LANES), lambda i: (0, i % nt, 0)),
            pl.BlockSpec((2, tm, LANES), lambda i: (0, i % nt, 0)),
        ],
        out_specs=[o[1] for o in outs],
        out_shape=[o[0] for o in outs],
        scratch_shapes=[pltpu.VMEM((bw // LANES, tm, LANES), F32)],
        compiler_params=pltpu.CompilerParams(dimension_semantics=("arbitrary",), vmem_limit_bytes=VMEM_LIMIT),
    )(x2d, g, w, bd, gains, cos_t, sin_t)


def _attn_kernel(*refs, n_seq, seq_len, tq, win, half, n_qp, n_kp, has_sink, emit_lse):
    it = iter(refs)
    q_all, k_all, v_all = next(it), next(it), next(it)
    sink_ref = next(it) if has_sink else None
    o_all = next(it)
    lse_all = next(it) if emit_lse else None
    lane = lax.broadcasted_iota(jnp.int32, (1, LANES), 1)
    low = lane < HEAD_DIM
    n_tiles = seq_len // tq

    total = max(n_seq, 1) * n_tiles
    staged = half is not None
    group = ATTN_INTERLEAVE if (staged and total % ATTN_INTERLEAVE == 0) else 1

    def tile_group(g, carry):
        for u in range(group):
            one_tile(g * group + u)
        return carry

    def one_tile(i):
        if n_seq:
            sq = i // n_tiles
            q_ref, k_ref, v_ref, o_ref = q_all.at[sq], k_all.at[sq], v_all.at[sq], o_all.at[sq]
            lse_ref = lse_all.at[sq] if emit_lse else None
            r0 = pl.multiple_of((i % n_tiles) * tq, tq)
        else:
            q_ref, k_ref, v_ref, o_ref, lse_ref = q_all, k_all, v_all, o_all, lse_all
            r0 = pl.multiple_of(i * tq, tq)
        if half is None or win == seq_len:
            ks = 0
        else:
            ks = pl.multiple_of(jnp.clip(r0 - half, 0, seq_len - win), half)
        if half is not None:
            d = (r0 - ks) + lax.broadcasted_iota(jnp.int32, (tq, win), 0) - lax.broadcasted_iota(jnp.int32, (tq, win), 1)
            visible = jnp.abs(d) <= half
        s_h, v_j = [], []
        for j in range(n_qp):
            kj = j if n_kp > 1 else 0
            q = q_ref[pl.ds(r0, tq), j * LANES:(j + 1) * LANES]
            k = k_ref[pl.ds(ks, win), kj * LANES:(kj + 1) * LANES]
            v_j.append(v_ref[pl.ds(ks, win), kj * LANES:(kj + 1) * LANES])
            for hf in range(2):
                qm = jnp.where(low if hf == 0 else jnp.logical_not(low), q, jnp.zeros_like(q))
                s = lax.dot_general(qm, k, (((1,), (1,)), ((), ())), preferred_element_type=F32)
                s_h.append(jnp.where(visible, s, NEG_INF) if half is not None else s)
        n_h = len(s_h)
        stages = [list(range(n_h))] if staged else [[h] for h in range(n_h)]
        o_h, lse_h = [None] * n_h, [None] * n_h
        for hs in stages:
            m_all = jnp.max(jnp.concatenate([s_h[h] for h in hs], axis=0), axis=-1, keepdims=True)
            m_h, p_h = {}, {}
            for n, h in enumerate(hs):
                m = m_all[n * tq:(n + 1) * tq]
                m_h[h] = jnp.maximum(m, sink_ref[h]) if has_sink else m
                p_h[h] = jnp.exp(s_h[h] - m_h[h])
            l_all = jnp.sum(jnp.concatenate([p_h[h] for h in hs], axis=0), axis=-1, keepdims=True)
            for n, h in enumerate(hs):
                l = l_all[n * tq:(n + 1) * tq]
                if has_sink:
                    l = l + jnp.exp(sink_ref[h] - m_h[h])
                o = jnp.dot(p_h[h].astype(BF16), v_j[h // 2], preferred_element_type=F32)
                o_h[h] = o / l
                if emit_lse:
                    lse_h[h] = m_h[h] + jnp.log(l)
        for j in range(n_qp):
            o_ref[pl.ds(r0, tq), j * LANES:(j + 1) * LANES] = jnp.where(low, o_h[2 * j], o_h[2 * j + 1])
            if emit_lse:
                lse_ref[pl.ds(r0, tq), j * LANES:(j + 1) * LANES] = jnp.where(low, lse_h[2 * j], lse_h[2 * j + 1])

    lax.fori_loop(0, total // group, tile_group, 0)


def _attn_call(name, q_arg, k_arg, v_arg, grid, q_spec, k_spec, v_spec, out_shape, out_spec, *,
               seq_len, tq, win, half, n_qp, n_kp, n_seq=0, sink=None, emit_lse=False):
    in_specs = [q_spec, k_spec, v_spec]
    args = [q_arg, k_arg, v_arg]
    if sink is not None:
        in_specs.append(pl.BlockSpec(memory_space=pltpu.SMEM))
        args.append(sink)
    n_out = 2 if emit_lse else 1
    res = pl.pallas_call(
        functools.partial(_attn_kernel, n_seq=n_seq, seq_len=seq_len, tq=tq, win=win, half=half, n_qp=n_qp, n_kp=n_kp,
                          has_sink=sink is not None, emit_lse=emit_lse),
        name=name,
        grid=grid,
        in_specs=in_specs,
        out_specs=[out_spec] * n_out,
        out_shape=[out_shape] * n_out,
        compiler_params=pltpu.CompilerParams(dimension_semantics=("arbitrary",) * len(grid),
                                             vmem_limit_bytes=VMEM_LIMIT),
    )(*args)
    return res


def _top_rows(s, payload=None):
    n_rows = s.shape[0]
    riota = lax.broadcasted_iota(jnp.int32, s.shape, 0).astype(F32)
    vals, picks = [], []
    for _ in range(PEER_TOPK):
        m = jnp.max(s, axis=0, keepdims=True)
        am = jnp.min(jnp.where(s == m, riota, float(n_rows)), axis=0, keepdims=True)
        hit = riota == am
        vals.append(m)
        if payload is None:
            picks.append(am)
        else:
            picks.append(jnp.max(jnp.where(hit, payload, -1.0), axis=0, keepdims=True))
        s = jnp.where(hit, -jnp.inf, s)
    return jnp.concatenate(vals, axis=0), jnp.concatenate(picks, axis=0)


def _pair_candidates(t0, t1, combine):
    h = PEER_TOPK // 2
    blocks = [combine(t0[0:1], t1)]
    blocks += [combine(t0[a:a + 1], t1[0:h]) for a in range(1, h)]
    blocks.append(combine(t0[h:], t1[0:1]))
    return jnp.concatenate(blocks, axis=0)


def _out_kernel(x_ref, oa_ref, ob1_ref, l1_ref, ob4_ref, l4_ref, ob16_ref, l16_ref, oc_ref,
                gn_ref, wo_ref, fn_ref, wq_ref, keys_ref,
                x1_ref, hn_ref, idx_ref, gate_ref,
                so4, sl4, so16, sl16, *, tm):
    ng = ob1_ref.shape[1] // LANES
    for r, o_src, l_src, o_dst, l_dst in ((4, ob4_ref, l4_ref, so4, sl4), (16, ob16_ref, l16_ref, so16, sl16)):
        for c in range(r):
            for g in range(ng):
                o_dst[g, pl.ds(c, tm // r, stride=r), :] = o_src[c, :, g * LANES:(g + 1) * LANES]
                l_dst[g, pl.ds(c, tm // r, stride=r), :] = l_src[c, :, g * LANES:(g + 1) * LANES]
    cat = lambda ref: jnp.concatenate([ref[g] for g in range(ng)], axis=-1)
    l1, l4, l16 = l1_ref[...], cat(sl4), cat(sl16)
    mx = jnp.maximum(jnp.maximum(l1, l4), l16)
    e1, e4, e16 = jnp.exp(l1 - mx), jnp.exp(l4 - mx), jnp.exp(l16 - mx)
    ob = (e1 * ob1_ref[...] + e4 * cat(so4) + e16 * cat(so16)) / (e1 + e4 + e16)
    mix = jnp.concatenate([_rms_rows(oa_ref[...]), _rms_rows(ob), _rms_rows(oc_ref[...])], axis=-1) * gn_ref[...]
    x1 = x_ref[...] + jnp.dot(mix.astype(BF16), wo_ref[...], preferred_element_type=F32)
    x1_ref[...] = x1
    hn = _rms_rows(x1) * fn_ref[...]
    hn_ref[...] = hn
    qp = jnp.dot(hn.astype(BF16), wq_ref[...], preferred_element_type=F32)

    idx_rows, gate_rows = [], []
    for h in range(PEER_HEADS):
        qh = qp[:, h * LANES:(h + 1) * LANES].astype(BF16)
        sv, si = [], []
        for p in range(2):
            st = lax.dot_general(keys_ref[h, p], qh, (((1,), (1,)), ((), ())), preferred_element_type=F32)
            v, i = _top_rows(st)
            sv.append(v)
            si.append(i)
        cand = _pair_candidates(sv[0], sv[1], lambda a, b: a + b)
        cidx = _pair_candidates(si[0], si[1], lambda a, b: a * float(N_KEYS) + b)
        best, eidx = _top_rows(cand, payload=cidx)
        e = jnp.exp(best - best[0:1])
        gate_rows.append(e / jnp.sum(e, axis=0, keepdims=True))
        idx_rows.append(eidx)
    idx_ref[...] = (jnp.concatenate(idx_rows, axis=0).T * float(EXPERT_ROWS)).astype(jnp.int32)
    gate_ref[...] = jnp.concatenate(gate_rows, axis=0).T


def _out_call(x2d, oa, ob1, l1, ob4, l4, ob16, l16, oc, gn, wo, fn, wq, keys_p, *, batch, seq):
    n, d = x2d.shape
    tm = TOKEN_TILE
    nt = seq // tm
    bw = 3 * LANES
    row = lambda i: (i, 0)
    const2 = lambda i: (0, 0)
    bmap = lambda i: (i // nt, 0, i % nt, 0)
    n_e = PEER_HEADS * PEER_TOPK
    return pl.pallas_call(
        functools.partial(_out_kernel, tm=tm),
        name="out_peerq",
        grid=(n // tm,),
        in_specs=[
            pl.BlockSpec((tm, d), row),
            pl.BlockSpec((tm, bw), row),
            pl.BlockSpec((None, None, tm, bw), bmap), pl.BlockSpec((None, None, tm, bw), bmap),
            pl.BlockSpec((None, 4, tm // 4, bw), bmap), pl.BlockSpec((None, 4, tm // 4, bw), bmap),
            pl.BlockSpec((None, 16, tm // 16, bw), bmap), pl.BlockSpec((None, 16, tm // 16, bw), bmap),
            pl.BlockSpec((tm, 2 * LANES), row),
            pl.BlockSpec(gn.shape, const2),
            pl.BlockSpec(wo.shape, const2),
            pl.BlockSpec(fn.shape, const2),
            pl.BlockSpec(wq.shape, const2),
            pl.BlockSpec(keys_p.shape, lambda i: (0, 0, 0, 0)),
        ],
        out_specs=[pl.BlockSpec((tm, d), row), pl.BlockSpec((tm, d), row),
                   pl.BlockSpec((tm, n_e), row), pl.BlockSpec((tm, n_e), row)],
        out_shape=[jax.ShapeDtypeStruct((n, d), F32), jax.ShapeDtypeStruct((n, d), F32),
                   jax.ShapeDtypeStruct((n, n_e), jnp.int32), jax.ShapeDtypeStruct((n, n_e), F32)],
        scratch_shapes=[pltpu.VMEM((bw // LANES, tm, LANES), F32)] * 4,
        compiler_params=pltpu.CompilerParams(dimension_semantics=("arbitrary",), vmem_limit_bytes=VMEM_LIMIT),
    )(x2d, oa, ob1, l1, ob4, l4, ob16, l16, oc, gn, wo, fn, wq, keys_p)


def _idx_pipeline(idx_hbm, idx_smem, sem, nh, body):
    i = pl.program_id(0)
    n = pl.num_programs(0)

    def copy(block, slot):
        return pltpu.make_async_copy(idx_hbm.at[pl.ds(block * nh, nh)], idx_smem.at[slot], sem.at[slot])

    @pl.when(i == 0)
    def _():
        copy(0, 0).start()

    copy(2 * i, 0).wait()
    copy(2 * i + 1, 1).start()
    body(0, idx_smem.at[0])
    copy(2 * i + 1, 1).wait()

    @pl.when(i + 1 < n)
    def _():
        copy(2 * i + 2, 0).start()

    body(1, idx_smem.at[1])


N_ROT = 3
SUB_ROWS = 2 * EXPERT_ROWS


def _expert_row(tbl_ref, row):
    return tbl_ref[pl.ds(pl.multiple_of(row, EXPERT_ROWS), EXPERT_ROWS), :]


def _sub_row_pick(ne):
    sub = lax.broadcasted_iota(jnp.int32, (SUB_ROWS, SUB_ROWS * ne), 0)
    col = lax.broadcasted_iota(jnp.int32, (SUB_ROWS, SUB_ROWS * ne), 1)
    return (col % SUB_ROWS) == (2 * (sub % EXPERT_ROWS) + sub // EXPERT_ROWS)


def _stage_experts(tbl_ref, idx_ref, tt, st_ref, ne):
    for k in range(ne):
        st_ref[pl.ds(EXPERT_ROWS * k, EXPERT_ROWS), :] = _expert_row(tbl_ref, idx_ref[tt, k])
    return pltpu.bitcast(st_ref[...], BF16)


def _peer_u_kernel(idx_hbm, hn_ref, gate_ref, tbl_ref, e_ref, et_ref, aexp_ref, idx_smem, sem, da_ref, db_ref,
                   *stage_refs, nt, ne):
    nh = nt // 2
    pick = _sub_row_pick(ne)
    d_ref = (da_ref, db_ref)

    def finish(hf):
        d = d_ref[hf][...]
        hi = d.astype(BF16)
        r1 = d - hi.astype(F32)
        mid = r1.astype(BF16)
        lo = (r1 - mid.astype(F32)).astype(BF16)
        s = jnp.dot(jnp.concatenate([hi, mid, lo], axis=1), et_ref[...], preferred_element_type=F32)
        gelu = 0.5 * s * (1.0 + lax.erf(s * (2.0 ** -0.5)))
        act = (gelu * gate_ref[pl.ds(hf * nh, nh), :]).astype(BF16)
        aexp_ref[pl.ds(hf * nh, nh), :] = jnp.dot(act, e_ref[...], preferred_element_type=F32)

    def half(hf, idx_ref):
        for tt in range(nh):
            t = hf * nh + tt
            rows = _stage_experts(tbl_ref, idx_ref, tt, stage_refs[tt % N_ROT], ne)
            x8 = hn_ref[pl.ds(t, 1), :].reshape(SUB_ROWS, LANES).astype(BF16)
            g = lax.dot_general(x8, rows, (((1,), (1,)), ((), ())), preferred_element_type=F32)
            d_ref[hf][pl.ds(tt, 1), :] = jnp.sum(jnp.where(pick, g, 0.0), axis=0, keepdims=True)
            if hf == 1 and tt == 0:
                finish(0)
        if hf == 1:
            finish(1)

    _idx_pipeline(idx_hbm, idx_smem, sem, nh, half)


def _peer_v_kernel(idx_hbm, aexp_ref, x_ref, tbl_ref, o_ref, idx_smem, sem, *stage_refs, nt, ne):
    nh = nt // 2
    pick = _sub_row_pick(ne)

    def half(hf, idx_ref):
        for tt in range(nh):
            t = hf * nh + tt
            rows = _stage_experts(tbl_ref, idx_ref, tt, stage_refs[tt % N_ROT], ne)
            a = jnp.where(pick, aexp_ref[pl.ds(t, 1), :], 0.0).astype(BF16)
            upd = jnp.dot(a, rows, preferred_element_type=F32)
            o_ref[pl.ds(t, 1), :] = x_ref[pl.ds(t, 1), :] + upd.reshape(1, SUB_ROWS * LANES)

    _idx_pipeline(idx_hbm, idx_smem, sem, nh, half)


def _peer_call(x1, hn, idx, gate, tbl_u, tbl_v):
    n, d = x1.shape
    ne = idx.shape[1]
    nt = PEER_TOKENS
    assert d == SUB_ROWS * LANES
    row2 = lambda i: (i, 0)
    tbl_spec = pl.BlockSpec(tbl_u.shape, lambda i: (0, 0), pipeline_mode=pl.Buffered(1))
    idx_spec = pl.BlockSpec(memory_space=pl.ANY)
    idx_scratch = [pltpu.SMEM((2, nt // 2, ne), jnp.int32), pltpu.SemaphoreType.DMA((2,))]
    params = pltpu.CompilerParams(dimension_semantics=("arbitrary",), vmem_limit_bytes=PEER_VMEM_LIMIT)
    wide = SUB_ROWS * ne
    expand = np.kron(np.eye(ne), np.ones((1, SUB_ROWS)))
    stage = [pltpu.VMEM((EXPERT_ROWS * ne, LANES), jnp.uint32)] * N_ROT
    aexp = pl.pallas_call(
        functools.partial(_peer_u_kernel, nt=nt, ne=ne),
        name="peer_u",
        grid=(n // nt,),
        in_specs=[idx_spec, pl.BlockSpec((nt, d), row2), pl.BlockSpec((nt, ne), row2), tbl_spec,
                  pl.BlockSpec((ne, wide), lambda i: (0, 0)), pl.BlockSpec((3 * wide, ne), lambda i: (0, 0))],
        out_specs=pl.BlockSpec((nt, wide), row2),
        out_shape=jax.ShapeDtypeStruct((n, wide), F32),
        scratch_shapes=idx_scratch + [pltpu.VMEM((nt // 2, wide), F32)] * 2 + stage,
        compiler_params=params,
    )(idx, hn, gate, tbl_u, jnp.asarray(expand, dtype=BF16),
      jnp.asarray(np.tile(expand.T, (3, 1)), dtype=BF16))
    out = pl.pallas_call(
        functools.partial(_peer_v_kernel, nt=nt, ne=ne),
        name="peer_v",
        grid=(n // nt,),
        in_specs=[idx_spec, pl.BlockSpec((nt, wide), row2), pl.BlockSpec((nt, d), row2), tbl_spec],
        out_specs=pl.BlockSpec((nt, d), row2),
        out_shape=jax.ShapeDtypeStruct((n, d), F32),
        scratch_shapes=idx_scratch + stage,
        compiler_params=params,
    )(idx, aexp, x1, tbl_v)
    return out


def _in_proj_perm():
    widths = (A_HEADS, A_KV_HEADS, A_KV_HEADS, B_HEADS, B_HEADS, B_HEADS, C_HEADS, C_KV_HEADS, C_KV_HEADS)
    orders = (A_Q_ORDER, None, None, None, None, None, C_Q_ORDER, None, None)
    cols, base = [], 0
    for w, order in zip(widths, orders):
        for hd in (order if order is not None else range(w)):
            cols.extend(range(base + hd * HEAD_DIM, base + (hd + 1) * HEAD_DIM))
        base += w * HEAD_DIM
    return np.asarray(cols, dtype=np.int32)


def _mix_perm():
    cols = []
    for hd in A_Q_ORDER:
        cols.extend(range(hd * HEAD_DIM, (hd + 1) * HEAD_DIM))
    base = A_HEADS * HEAD_DIM
    cols.extend(range(base, base + B_HEADS * HEAD_DIM))
    base += B_HEADS * HEAD_DIM
    for hd in C_Q_ORDER:
        cols.extend(range(base + hd * HEAD_DIM, base + (hd + 1) * HEAD_DIM))
    return np.asarray(cols, dtype=np.int32)


def _rope_tables(seq):
    pos = jnp.arange(seq, dtype=F32)
    f64 = ROPE_THETA ** (-jnp.arange(0, HEAD_DIM, 2, dtype=F32) / HEAD_DIM)
    ang = pos[:, None] * f64[None, :]
    cos1 = jnp.concatenate([jnp.cos(ang), jnp.cos(ang)], axis=-1)
    sin1 = jnp.concatenate([-jnp.sin(ang), jnp.sin(ang)], axis=-1)
    half = HEAD_DIM // 2
    f32_ = ROPE_THETA ** (-jnp.arange(0, half, 2, dtype=F32) / half)
    rows = seq // GRID_W
    row = jnp.repeat(jnp.arange(rows, dtype=F32), GRID_W)
    col = jnp.tile(jnp.arange(GRID_W, dtype=F32), rows)
    ar, ac = row[:, None] * f32_[None, :], col[:, None] * f32_[None, :]
    cos_a = jnp.concatenate([jnp.cos(ar), jnp.cos(ar), jnp.cos(ac), jnp.cos(ac)], axis=-1)
    sin_a = jnp.concatenate([-jnp.sin(ar), jnp.sin(ar), -jnp.sin(ac), jnp.sin(ac)], axis=-1)
    two = lambda t: jnp.concatenate([t, t], axis=-1)
    return jnp.stack([two(cos_a), two(cos1)]), jnp.stack([two(sin_a), two(sin1)])


def _group_gains(g):
    rows = []
    mix_id = {"a": 0, "b": 1, "c": 2}
    for mixer, role, kind in _GROUPS:
        if kind is None:
            rows.append(jnp.ones((LANES,), F32))
        else:
            gg = g[mix_id[mixer], 0 if role == "q" else 1].astype(F32)
            if role == "q":
                gg = gg * ATTN_SCALE
            rows.append(jnp.concatenate([gg, gg]))
    return jnp.stack(rows)


def _pack_table(t):
    e, d = t.shape
    assert d == 2 * EXPERT_ROWS * LANES
    b = lax.bitcast_convert_type(t.astype(BF16), jnp.uint16).astype(jnp.uint32)
    w = (b[:, d // 2:] << 16) | b[:, :d // 2]
    return w.reshape(e * EXPERT_ROWS, LANES)


def _pad_keys(keys):
    z = jnp.zeros_like(keys[:, 0])
    k0 = jnp.concatenate([keys[:, 0], z], axis=-1)
    k1 = jnp.concatenate([z, keys[:, 1]], axis=-1)
    return jnp.stack([k0, k1], axis=1).astype(BF16)


def _layer(x2d, batch, seq, g_attn, w_in, gains, cos_t, sin_t, bd, sink, g_out, w_out, g_ffn, wq, keys_p,
           tbl_u, tbl_v):
    n = x2d.shape[0]
    aq, ak, av, b1, b4, b16, cq, ck, cv = _proj_call(x2d, g_attn, w_in, bd, gains, cos_t, sin_t,
                                                     batch=batch, seq=seq)
    oa = _attn_call(
        "attn_a", aq, ak, av, (batch,),
        pl.BlockSpec((seq, 3 * LANES), lambda b: (b, 0)),
        pl.BlockSpec((seq, LANES), lambda b: (b, 0)),
        pl.BlockSpec((seq, LANES), lambda b: (b, 0)),
        jax.ShapeDtypeStruct((n, 3 * LANES), F32), pl.BlockSpec((seq, 3 * LANES), lambda b: (b, 0)),
        seq_len=seq, tq=256, win=seq, half=None, n_qp=3, n_kp=1)[0]
    ob, lse = [], []
    for (window, r), arr in zip(DILATIONS, (b1, b4, b16)):
        length = seq // r
        half = window // (2 * r)
        tq = min(128, length)
        win = min(tq + 2 * half, length)
        spec = lambda c: pl.BlockSpec((None, r, length, 3 * LANES), lambda b, c=c: (b, 0, 0, c))
        o_r, l_r = _attn_call(
            f"attn_b{r}", arr, arr, arr, (batch,), spec(0), spec(1), spec(2),
            jax.ShapeDtypeStruct((batch, r, length, 3 * LANES), F32), spec(0),
            seq_len=length, tq=tq, win=win, half=half, n_qp=3, n_kp=3, n_seq=r, emit_lse=True)
        ob.append(o_r)
        lse.append(l_r)
    tq = LOCAL_HALF_WINDOW
    oc = _attn_call(
        "attn_c", cq, ck, cv, (batch,),
        pl.BlockSpec((seq, 2 * LANES), lambda b: (b, 0)),
        pl.BlockSpec((seq, LANES), lambda b: (b, 0)),
        pl.BlockSpec((seq, LANES), lambda b: (b, 0)),
        jax.ShapeDtypeStruct((n, 2 * LANES), F32), pl.BlockSpec((seq, 2 * LANES), lambda b: (b, 0)),
        seq_len=seq, tq=tq, win=min(tq + 2 * LOCAL_HALF_WINDOW, seq), half=LOCAL_HALF_WINDOW,
        n_qp=2, n_kp=1, sink=sink)[0]
    x1, hn, idx, gate = _out_call(x2d, oa, ob[0], lse[0], ob[1], lse[1], ob[2], lse[2], oc,
                                  g_out, w_out, g_ffn, wq, keys_p, batch=batch, seq=seq)
    return _peer_call(x1, hn, idx, gate, tbl_u, tbl_v)


def kernel(x, attn_norm, w_in, qk_gain, sink_logits, out_norm, w_out, ffn_norm, peer_wq, peer_keys, peer_u, peer_v):
    batch, seq, d = x.shape
    depth = w_in.shape[0]
    in_perm = _in_proj_perm()
    mix_perm = _mix_perm()
    cos_t, sin_t = _rope_tables(seq)
    bd = jnp.asarray(np.kron(np.eye(2), np.ones((HEAD_DIM, HEAD_DIM))), dtype=BF16)
    sink_perm = np.asarray([C_Q_ORDER[2 * j + hf] for j in range(2) for hf in range(2)], dtype=np.int32)
    x2d = x.reshape(batch * seq, d)
    for layer in range(depth):
        x2d = _layer(
            x2d, batch, seq,
            attn_norm[layer][None, :].astype(F32),
            w_in[layer][:, in_perm].astype(BF16),
            _group_gains(qk_gain[layer]),
            cos_t, sin_t, bd,
            sink_logits[layer][sink_perm].astype(F32),
            out_norm[layer][mix_perm][None, :].astype(F32),
            w_out[layer][mix_perm, :].astype(BF16),
            ffn_norm[layer][None, :].astype(F32),
            peer_wq[layer].astype(BF16),
            _pad_keys(peer_keys[layer]),
            _pack_table(peer_u[layer]),
            _pack_table(peer_v[layer]),
        )
    return x2d.reshape(batch, seq, d)
```
